```python
import math
import jax, jax.numpy as jnp
from jax import lax
import numpy as np

D_MODEL = 1024
BATCH = 2
SEQ = 8192
DEPTH = 2
DEC_BATCH = 32
DEC_SEQ = 8
PAST_LEN = 16384
PAGE_SIZE = 128

N_A_LAYERS = DEPTH // 2
D_MIX = D_MODEL
D_MEM = D_MODEL // 4
D_MAIN = D_MIX - D_MEM
MEM_HEADS = 4
MEM_HEAD_DIM = D_MEM // MEM_HEADS
MEM_TOKENS = 256
S5_GROUP_CH = 16
S5_GROUPS = D_MAIN // S5_GROUP_CH
S5_STATE = 64
S5_CHUNK = 128
FOX_HEAD_DIM = 64
FOX_HEADS = D_MAIN // FOX_HEAD_DIM
Q_BLOCK = 128
D_FF = 4 * D_MODEL
RMS_EPS = 1e-6
NEG_INF = -1e30

kernel_name = "yoco_s5_fox_hybrid_step"


def rmsnorm(x, g):
    x32 = x.astype(jnp.float32)
    y = x32 * lax.rsqrt(jnp.mean(x32 * x32, axis=-1, keepdims=True) + RMS_EPS)
    return (y * g.astype(jnp.float32)).astype(x.dtype)


def sqrelu_mlp(x, w_up, w_down):
    h = jax.nn.relu(x @ w_up)
    return (h * h) @ w_down


def memory_kv(mem, w):
    n, m, _ = mem.shape
    kv = mem @ w
    k = kv[..., :D_MEM].reshape(n, m, MEM_HEADS, MEM_HEAD_DIM)
    v = kv[..., D_MEM:].reshape(n, m, MEM_HEADS, MEM_HEAD_DIM)
    return k, v


def memory_attend(q, mk, mv):
    s = jnp.einsum("blhd,bmhd->bhlm", q, mk).astype(jnp.float32) * (MEM_HEAD_DIM ** -0.5)
    p = jax.nn.softmax(s, axis=-1)
    return jnp.einsum("bhlm,bmhd->blhd", p.astype(mv.dtype), mv)


def s5_discretize(a_re, a_im, log_step, b_re, b_im):
    dt = jnp.exp(log_step.astype(jnp.float32))[:, None]
    a_re = a_re.astype(jnp.float32)
    a_im = a_im.astype(jnp.float32)
    mag = jnp.exp(a_re * dt)
    ab_re = mag * jnp.cos(a_im * dt)
    ab_im = mag * jnp.sin(a_im * dt)
    den = a_re * a_re + a_im * a_im
    nr = ab_re - 1.0
    ni = ab_im
    cr = (nr * a_re + ni * a_im) / den
    ci = (ni * a_re - nr * a_im) / den
    b_re = b_re.astype(jnp.float32)
    b_im = b_im.astype(jnp.float32)
    bb_re = cr[..., None] * b_re - ci[..., None] * b_im
    bb_im = cr[..., None] * b_im + ci[..., None] * b_re
    return ab_re, ab_im, bb_re, bb_im


def _ssm_combine(c1, c2):
    a1r, a1i, b1r, b1i = c1
    a2r, a2i, b2r, b2i = c2
    return (a2r * a1r - a2i * a1i,
            a2r * a1i + a2i * a1r,
            a2r * b1r - a2i * b1i + b2r,
            a2r * b1i + a2i * b1r + b2i)


def s5_mixer(u, h0_re, h0_im, a_re, a_im, log_step, b_re, b_im, c_re, c_im, d, w_glu, b_glu):
    n, l, _ = u.shape
    u32 = u.astype(jnp.float32).reshape(n, l, S5_GROUPS, S5_GROUP_CH)
    ab_re, ab_im, bb_re, bb_im = s5_discretize(a_re, a_im, log_step, b_re, b_im)
    c_re = c_re.astype(jnp.float32)
    c_im = c_im.astype(jnp.float32)
    d = d.astype(jnp.float32)
    t = S5_CHUNK if l % S5_CHUNK == 0 else l
    nc = l // t
    u_blocks = u32.reshape(n, nc, t, S5_GROUPS, S5_GROUP_CH).swapaxes(0, 1)

    def step(carry, u_t):
        hr, hi = carry
        bur = jnp.einsum("ntgc,gpc->ntgp", u_t, bb_re)
        bui = jnp.einsum("ntgc,gpc->ntgp", u_t, bb_im)
        bur = bur.at[:, 0].add(ab_re * hr - ab_im * hi)
        bui = bui.at[:, 0].add(ab_re * hi + ab_im * hr)
        ar = jnp.broadcast_to(ab_re, bur.shape)
        ai = jnp.broadcast_to(ab_im, bur.shape)
        _, _, sr, si = lax.associative_scan(_ssm_combine, (ar, ai, bur, bui), axis=1)
        y = (jnp.einsum("ntgp,gcp->ntgc", sr, c_re) - jnp.einsum("ntgp,gcp->ntgc", si, c_im)
             + d * u_t)
        return (sr[:, -1], si[:, -1]), y

    (hr, hi), y = lax.scan(step, (h0_re.astype(jnp.float32), h0_im.astype(jnp.float32)), u_blocks)
    y = y.swapaxes(0, 1).reshape(n, l, D_MAIN)
    z = jax.nn.gelu(y)
    out = z * jax.nn.sigmoid(z @ w_glu.astype(jnp.float32) + b_glu.astype(jnp.float32))
    return out.astype(u.dtype), hr, hi


def shared_kv(h, norm_kv, w_kv, w_f, b_f):
    n, l, _ = h.shape
    hn = rmsnorm(h, norm_kv)
    kv = hn @ w_kv
    k = kv[..., :D_MAIN].reshape(n, l, FOX_HEADS, FOX_HEAD_DIM)
    v = kv[..., D_MAIN:].reshape(n, l, FOX_HEADS, FOX_HEAD_DIM)
    logf = jax.nn.log_sigmoid((hn @ w_f + b_f).astype(jnp.float32))
    return k, v, logf


def fox_attend(q, f_q, q_pos, groups):
    n, lq, h, dh = q.shape
    qb = Q_BLOCK if lq % Q_BLOCK == 0 else lq
    nb = lq // qb
    scale = dh ** -0.5
    groups_t = tuple((k, v, jnp.swapaxes(f_k, 1, 2), k_pos) for (k, v, f_k, k_pos) in groups)
    q_blk = q.reshape(n, nb, qb, h, dh).swapaxes(0, 1)
    fq_blk = jnp.swapaxes(f_q, 1, 2).reshape(n, h, nb, qb).transpose(2, 0, 1, 3)
    pos_blk = q_pos.reshape(nb, qb)

    def one_block(args):
        q_b, fq_b, pos_b = args
        logits = []
        for k, _, fk, k_pos in groups_t:
            s = jnp.einsum("bqhd,bkhd->bhqk", q_b, k).astype(jnp.float32) * scale
            s = s + fq_b[..., :, None] - fk[:, :, None, :]
            logits.append(jnp.where(k_pos[None, None, None, :] <= pos_b[None, None, :, None], s, NEG_INF))
        probs = jax.nn.softmax(jnp.concatenate(logits, axis=-1), axis=-1)
        out = None
        start = 0
        for k, v, _, _ in groups_t:
            lk = k.shape[1]
            o = jnp.einsum("bhqk,bkhd->bqhd", probs[..., start:start + lk].astype(v.dtype), v)
            out = o if out is None else out + o
            start += lk
        return out

    out = lax.map(one_block, (q_blk, fq_blk, pos_blk))
    return out.swapaxes(0, 1).reshape(n, lq, h, dh)


def trunk(x, pos0, h0_re, h0_im, mem_k, mem_v, past, p):
    n, l, _ = x.shape
    q_pos = pos0 + jnp.arange(l, dtype=jnp.int32)
    h = x
    s5_re, s5_im = [], []
    k_new = v_new = logf_new = None
    f_q = None
    groups = None
    for i in range(DEPTH):
        z = rmsnorm(h, p["norm_mix"][i]) @ p["w_in"][i]
        q_mem = z[..., D_MAIN:].reshape(n, l, MEM_HEADS, MEM_HEAD_DIM)
        y_mem = memory_attend(q_mem, mem_k[i], mem_v[i]).reshape(n, l, D_MEM)
        if i < N_A_LAYERS:
            y_main, hr, hi = s5_mixer(z[..., :D_MAIN], h0_re[i], h0_im[i],
                                      p["s5_a_re"][i], p["s5_a_im"][i], p["s5_log_step"][i],
                                      p["s5_b_re"][i], p["s5_b_im"][i], p["s5_c_re"][i], p["s5_c_im"][i],
                                      p["s5_d"][i], p["s5_w_glu"][i], p["s5_b_glu"][i])
            s5_re.append(hr)
            s5_im.append(hi)
        else:
            if i == N_A_LAYERS:
                k_new, v_new, logf_new = shared_kv(h, p["norm_kv"], p["w_kv"], p["w_f"], p["b_f"])
                if past is None:
                    f_q = jnp.cumsum(logf_new, axis=1)
                    groups = ((k_new, v_new, f_q, q_pos),)
                else:
                    k_past, v_past, logf_past = past
                    past_len = k_past.shape[1]
                    f_all = jnp.cumsum(jnp.concatenate([logf_past.astype(jnp.float32), logf_new], axis=1), axis=1)
                    f_q = f_all[:, past_len:]
                    groups = ((k_past, v_past, f_all[:, :past_len], jnp.arange(past_len, dtype=jnp.int32)),
                              (k_new, v_new, f_q, q_pos))
            q = z[..., :D_MAIN].reshape(n, l, FOX_HEADS, FOX_HEAD_DIM)
            y_main = fox_attend(q, f_q, q_pos, groups).reshape(n, l, D_MAIN)
        h = h + jnp.concatenate([y_main, y_mem], axis=-1) @ p["w_out"][i]
        h = h + sqrelu_mlp(rmsnorm(h, p["norm_mlp"][i]), p["w_up"][i], p["w_down"][i])
    y = rmsnorm(h, p["norm_final"])
    return y, jnp.stack(s5_re), jnp.stack(s5_im), k_new, v_new, logf_new


def setup_inputs(seed: int = 0) -> dict:
    key = jax.random.key(seed)
    ks = jax.random.split(key, 33)
    f32 = jnp.float32

    def normal(i, shape, scale=1.0):
        return scale * jax.random.normal(ks[i], shape, f32)

    n_pages = PAST_LEN // PAGE_SIZE
    n_used = DEC_BATCH * n_pages
    n_phys = (n_used * 5) // 4
    page_table = jax.random.permutation(ks[9], n_phys)[:n_used].reshape(DEC_BATCH, n_pages).astype(jnp.int32)
    s5_shape = (N_A_LAYERS, S5_GROUPS, S5_STATE)
    return {
        "x_prompt": normal(0, (BATCH, SEQ, D_MODEL)),
        "x_sample": normal(1, (DEC_BATCH, DEC_SEQ, D_MODEL)),
        "state_s5_re": normal(2, (N_A_LAYERS, DEC_BATCH, S5_GROUPS, S5_STATE), 0.5),
        "state_s5_im": normal(3, (N_A_LAYERS, DEC_BATCH, S5_GROUPS, S5_STATE), 0.5),
        "cache_mem_k": normal(4, (DEPTH, DEC_BATCH, MEM_TOKENS, MEM_HEADS, MEM_HEAD_DIM)),
        "cache_mem_v": normal(5, (DEPTH, DEC_BATCH, MEM_TOKENS, MEM_HEADS, MEM_HEAD_DIM)),
        "cache_k": normal(6, (n_phys, PAGE_SIZE, FOX_HEADS, FOX_HEAD_DIM)),
        "cache_v": normal(7, (n_phys, PAGE_SIZE, FOX_HEADS, FOX_HEAD_DIM)),
        "cache_logf": jax.nn.log_sigmoid(2.5 + normal(8, (n_phys, PAGE_SIZE, FOX_HEADS))),
        "page_table": page_table,
        "mem_prompt": normal(10, (BATCH, MEM_TOKENS, D_MODEL)),
        "norm_mix": 1.0 + normal(11, (DEPTH, D_MODEL), 0.1),
        "norm_mlp": 1.0 + normal(12, (DEPTH, D_MODEL), 0.1),
        "w_in": normal(13, (DEPTH, D_MODEL, D_MIX), D_MODEL ** -0.5),
        "w_out": normal(14, (DEPTH, D_MIX, D_MODEL), D_MIX ** -0.5),
        "w_up": normal(15, (DEPTH, D_MODEL, D_FF), D_MODEL ** -0.5),
        "w_down": normal(16, (DEPTH, D_FF, D_MODEL), D_FF ** -0.5),
        "w_mem_kv": normal(17, (DEPTH, D_MODEL, 2 * D_MEM), D_MODEL ** -0.5),
        "s5_a_re": -0.5 + normal(18, s5_shape, 0.01),
        "s5_a_im": math.pi * jnp.arange(S5_STATE, dtype=f32) + normal(19, s5_shape, 0.01),
        "s5_log_step": jax.random.uniform(ks[20], (N_A_LAYERS, S5_GROUPS), f32,
                                          minval=math.log(1e-3), maxval=math.log(1e-1)),
        "s5_b_re": normal(21, (N_A_LAYERS, S5_GROUPS, S5_STATE, S5_GROUP_CH), (2 * S5_GROUP_CH) ** -0.5),
        "s5_b_im": normal(22, (N_A_LAYERS, S5_GROUPS, S5_STATE, S5_GROUP_CH), (2 * S5_GROUP_CH) ** -0.5),
        "s5_c_re": normal(23, (N_A_LAYERS, S5_GROUPS, S5_GROUP_CH, S5_STATE), S5_STATE ** -0.5),
        "s5_c_im": normal(24, (N_A_LAYERS, S5_GROUPS, S5_GROUP_CH, S5_STATE), S5_STATE ** -0.5),
        "s5_d": normal(25, (N_A_LAYERS, S5_GROUPS, S5_GROUP_CH)),
        "s5_w_glu": normal(26, (N_A_LAYERS, D_MAIN, D_MAIN), D_MAIN ** -0.5),
        "s5_b_glu": normal(27, (N_A_LAYERS, D_MAIN), 0.01),
        "norm_kv": 1.0 + normal(28, (D_MODEL,), 0.1),
        "w_kv": normal(29, (D_MODEL, 2 * D_MAIN), D_MODEL ** -0.5),
        "w_f": normal(30, (D_MODEL, FOX_HEADS), D_MODEL ** -0.5),
        "b_f": jax.random.uniform(ks[31], (FOX_HEADS,), f32, minval=1.0, maxval=4.0),
        "norm_final": 1.0 + normal(32, (D_MODEL,), 0.1),
    }


def reference(x_prompt, x_sample, state_s5_re, state_s5_im, cache_mem_k, cache_mem_v,
              cache_k, cache_v, cache_logf, page_table, mem_prompt,
              norm_mix, norm_mlp, w_in, w_out, w_up, w_down, w_mem_kv,
              s5_a_re, s5_a_im, s5_log_step, s5_b_re, s5_b_im, s5_c_re, s5_c_im, s5_d,
              s5_w_glu, s5_b_glu, norm_kv, w_kv, w_f, b_f, norm_final):
    p = dict(norm_mix=norm_mix, norm_mlp=norm_mlp, w_in=w_in, w_out=w_out, w_up=w_up, w_down=w_down,
             s5_a_re=s5_a_re, s5_a_im=s5_a_im, s5_log_step=s5_log_step, s5_b_re=s5_b_re, s5_b_im=s5_b_im,
             s5_c_re=s5_c_re, s5_c_im=s5_c_im, s5_d=s5_d, s5_w_glu=s5_w_glu, s5_b_glu=s5_b_glu,
             norm_kv=norm_kv, w_kv=w_kv, w_f=w_f, b_f=b_f, norm_final=norm_final)

    n_p = x_prompt.shape[0]
    mem_kv_p = [memory_kv(mem_prompt, w_mem_kv[i]) for i in range(DEPTH)]
    p_mem_k = jnp.stack([kv[0] for kv in mem_kv_p])
    p_mem_v = jnp.stack([kv[1] for kv in mem_kv_p])
    zeros = jnp.zeros((N_A_LAYERS, n_p, S5_GROUPS, S5_STATE), jnp.float32)
    y_prompt, p_s5_re, p_s5_im, p_k, p_v, p_logf = trunk(
        x_prompt, 0, zeros, zeros, p_mem_k, p_mem_v, None, p)

    n_s = x_sample.shape[0]
    n_pages = page_table.shape[1]
    past_len = n_pages * cache_k.shape[1]
    k_past = cache_k[page_table].reshape(n_s, past_len, FOX_HEADS, FOX_HEAD_DIM)
    v_past = cache_v[page_table].reshape(n_s, past_len, FOX_HEADS, FOX_HEAD_DIM)
    logf_past = cache_logf[page_table].reshape(n_s, past_len, FOX_HEADS)
    y_sample, s_s5_re, s_s5_im, s_k, s_v, s_logf = trunk(
        x_sample, past_len, state_s5_re, state_s5_im, cache_mem_k, cache_mem_v,
        (k_past, v_past, logf_past), p)

    return (y_prompt, y_sample, p_s5_re, p_s5_im, p_mem_k, p_mem_v, p_k, p_v, p_logf,
            s_s5_re, s_s5_im, s_k, s_v, s_logf)
```

```python
import functools
import math

import jax
import jax.numpy as jnp
from jax import lax
from jax.experimental import pallas as pl
from jax.experimental.pallas import tpu as pltpu

BF = jnp.bfloat16
F32 = jnp.float32
RMS_EPS = 1e-6
NEG_INF = -1e30
V7X_VMEM_BYTES = 64 * 1024 * 1024
VMEM_LIMIT = V7X_VMEM_BYTES - 8 * 1024 * 1024
LANES = 128
SUBLANES = 8
S5_CHUNK = 32
HEAD_PAD = 16
DEC_PAGES_PER_STEP = 8
LOGF_PAGES_PER_STEP = 16

NT_DIMS = (((1,), (1,)), ((), ()))


def _params(*sem):
    return pltpu.CompilerParams(dimension_semantics=sem, vmem_limit_bytes=VMEM_LIMIT)


def _dot(a, b):
    return jnp.dot(a, b, preferred_element_type=F32)


def _dot_nt(a, b):
    return lax.dot_general(a, b, NT_DIMS, preferred_element_type=F32)


def _rms(x, g):
    ms = jnp.mean(x * x, axis=-1, keepdims=True)
    return x * lax.rsqrt(ms + RMS_EPS) * g


def _split3(x):
    hi = x.astype(BF)
    r1 = x - hi.astype(F32)
    mid = r1.astype(BF)
    lo = (r1 - mid.astype(F32)).astype(BF)
    return hi, mid, lo


def _rms_matmul_body(x_ref, g_ref, w_ref, o_ref):
    xn = _rms(x_ref[...], g_ref[...])
    o_ref[...] = _dot(xn.astype(BF), w_ref[...])


def rms_matmul(x, g, w, *, tm=512):
    t, d = x.shape
    n = w.shape[1]
    tm = min(tm, t)
    return pl.pallas_call(
        _rms_matmul_body,
        grid=(t // tm,),
        in_specs=[pl.BlockSpec((tm, d), lambda i: (i, 0)),
                  pl.BlockSpec((1, d), lambda i: (0, 0)),
                  pl.BlockSpec((d, n), lambda i: (0, 0))],
        out_specs=pl.BlockSpec((tm, n), lambda i: (i, 0)),
        out_shape=jax.ShapeDtypeStruct((t, n), F32),
        compiler_params=_params("parallel"),
        name="rms_matmul",
    )(x, g.reshape(1, d), w)


def _memkv_body(w_ref, m_ref, o_ref):
    o_ref[0, 0] = _dot_nt(w_ref[0], m_ref[0].astype(BF))


def memory_kv_t(mem, w_t):
    nb, m, d = mem.shape
    depth, f, _ = w_t.shape
    return pl.pallas_call(
        _memkv_body,
        grid=(depth, nb),
        in_specs=[pl.BlockSpec((1, f, d), lambda l, b: (l, 0, 0)),
                  pl.BlockSpec((1, m, d), lambda l, b: (b, 0, 0))],
        out_specs=pl.BlockSpec((1, 1, f, m), lambda l, b: (l, b, 0, 0)),
        out_shape=jax.ShapeDtypeStruct((depth, nb, f, m), F32),
        compiler_params=_params("parallel", "parallel"),
        name="memory_kv",
    )(w_t, mem)


def _memattn_body(q_ref, kt_ref, vt_ref, o_ref, *, heads, dh):
    q = q_ref[0] * (dh ** -0.5)
    kt = kt_ref[0].astype(BF)
    vt = vt_ref[0].astype(BF)
    lane = lax.broadcasted_iota(jnp.int32, q.shape, 1)
    out = jnp.zeros(q.shape, F32)
    for h in range(heads):
        sel = (lane >= h * dh) & (lane < (h + 1) * dh)
        s = _dot(jnp.where(sel, q, 0.0).astype(BF), kt)
        m = jnp.max(s, axis=-1, keepdims=True)
        p = jnp.exp(s - m)
        l = jnp.sum(p, axis=-1, keepdims=True)
        o = _dot_nt((p / l).astype(BF), vt)
        out = jnp.where(sel, o, out)
    o_ref[0] = out


def memory_attend(z, kt, vt, *, heads, col_block, tm=512):
    n, l, _ = z.shape
    f, m = kt.shape[1:]
    tm = min(tm, l)
    body = functools.partial(_memattn_body, heads=heads, dh=f // heads)
    return pl.pallas_call(
        body,
        grid=(n, l // tm),
        in_specs=[pl.BlockSpec((1, tm, f), lambda b, i: (b, i, col_block)),
                  pl.BlockSpec((1, f, m), lambda b, i: (b, 0, 0)),
                  pl.BlockSpec((1, f, m), lambda b, i: (b, 0, 0))],
        out_specs=pl.BlockSpec((1, tm, f), lambda b, i: (b, i, 0)),
        out_shape=jax.ShapeDtypeStruct((n, l, f), F32),
        compiler_params=_params("parallel", "parallel"),
        name="memory_attend",
    )(z, kt, vt)


def _cmul(a_ref_row, h):
    p = h.shape[-1] // 2
    return a_ref_row[0:1, :] * h + a_ref_row[1:2, :] * pltpu.roll(h, p, axis=1)


def _s5_body(u_ref, km_ref, wst_ref, wout_ref, apow_ref, h0_ref, y_ref, hend_ref, *, nseq, rows):
    u = u_ref[0].astype(BF)
    y = _dot(u, km_ref[0])
    hs = _dot(u, wst_ref[0])
    h0 = h0_ref[0]
    a_chunk = apow_ref[0, 0]
    if rows == 1:
        hs = hs + _cmul(a_chunk, h0)
        h_in = h0
    else:
        rid = lax.broadcasted_iota(jnp.int32, hs.shape, 0)
        rloc = rid % rows
        inject = jnp.zeros(hs.shape, F32)
        for b in range(nseq):
            inject = jnp.where(rid == b * rows, h0[b:b + 1, :], inject)
        hs = hs + jnp.where(rloc == 0, _cmul(a_chunk, inject), 0.0)
        step = 1
        k = 0
        while step < rows:
            shifted = jnp.where(rloc >= step, pltpu.roll(hs, step, axis=0), 0.0)
            hs = hs + _cmul(apow_ref[0, k], shifted)
            step *= 2
            k += 1
        h_in = jnp.where(rloc == 0, inject, pltpu.roll(hs, 1, axis=0))
    y_ref[0] = y + _dot(h_in.astype(BF), wout_ref[0])
    for b in range(nseq):
        hend_ref[0, b:b + 1, :] = hs[(b + 1) * rows - 1:(b + 1) * rows, :]


def s5_core(u_g, km, wst, wout, apow, h0, *, nseq, rows):
    g, nr, tc = u_g.shape
    p2 = wst.shape[2]
    nk = apow.shape[1]
    body = functools.partial(_s5_body, nseq=nseq, rows=rows)
    return pl.pallas_call(
        body,
        grid=(g,),
        in_specs=[pl.BlockSpec((1, nr, tc), lambda i: (i, 0, 0)),
                  pl.BlockSpec((1, tc, tc), lambda i: (i, 0, 0)),
                  pl.BlockSpec((1, tc, p2), lambda i: (i, 0, 0)),
                  pl.BlockSpec((1, p2, tc), lambda i: (i, 0, 0)),
                  pl.BlockSpec((1, nk, 2, p2), lambda i: (i, 0, 0, 0)),
                  pl.BlockSpec((1, nseq, p2), lambda i: (i, 0, 0))],
        out_specs=[pl.BlockSpec((1, nr, tc), lambda i: (i, 0, 0)),
                   pl.BlockSpec((1, nseq, p2), lambda i: (i, 0, 0))],
        out_shape=[jax.ShapeDtypeStruct((g, nr, tc), F32),
                   jax.ShapeDtypeStruct((g, nseq, p2), F32)],
        compiler_params=_params("parallel"),
        name="s5_core",
    )(u_g, km, wst, wout, apow, h0)


def s5_operators(a_re, a_im, log_step, b_re, b_im, c_re, c_im, d, *, chunk, rows):
    hp = lax.Precision.HIGHEST
    g, p = a_re.shape
    c = b_re.shape[-1]
    dt = jnp.exp(log_step.astype(F32))[:, None]
    a_re = a_re.astype(F32)
    a_im = a_im.astype(F32)
    lam_re = a_re * dt
    lam_im = a_im * dt
    mag = jnp.exp(lam_re)
    ab_re = mag * jnp.cos(lam_im)
    ab_im = mag * jnp.sin(lam_im)
    den = a_re * a_re + a_im * a_im
    nr = ab_re - 1.0
    ni = ab_im
    cr = (nr * a_re + ni * a_im) / den
    ci = (ni * a_re - nr * a_im) / den
    bb_re = cr[..., None] * b_re - ci[..., None] * b_im
    bb_im = cr[..., None] * b_im + ci[..., None] * b_re

    def power(tau):
        tau = tau.astype(F32)[None, :, None]
        m = jnp.exp(lam_re[:, None, :] * tau)
        ph = lam_im[:, None, :] * tau
        return m * jnp.cos(ph), m * jnp.sin(ph)

    pw_re, pw_im = power(jnp.arange(chunk + 1))
    e_re = pw_re[..., None] * bb_re[:, None] - pw_im[..., None] * bb_im[:, None]
    e_im = pw_re[..., None] * bb_im[:, None] + pw_im[..., None] * bb_re[:, None]
    ktau = (jnp.einsum("gop,gtpi->gtoi", c_re, e_re[:, :chunk], precision=hp)
            - jnp.einsum("gop,gtpi->gtoi", c_im, e_im[:, :chunk], precision=hp))
    ktau = ktau.at[:, 0].add(d[:, :, None] * jnp.eye(c, dtype=F32)[None])
    s_idx = jnp.arange(chunk)[:, None]
    t_idx = jnp.arange(chunk)[None, :]
    lag = t_idx - s_idx
    kfull = jnp.where((lag >= 0)[None, :, :, None, None], ktau[:, jnp.maximum(lag, 0)], 0.0)
    km = kfull.transpose(0, 1, 4, 2, 3).reshape(g, chunk * c, chunk * c)

    rev = chunk - 1 - jnp.arange(chunk)
    wst = jnp.concatenate([e_re[:, rev].transpose(0, 1, 3, 2), e_im[:, rev].transpose(0, 1, 3, 2)],
                          axis=-1).reshape(g, chunk * c, 2 * p)
    q_re = pw_re[:, 1:chunk + 1]
    q_im = pw_im[:, 1:chunk + 1]
    wo_re = c_re[:, None] * q_re[:, :, None, :] - c_im[:, None] * q_im[:, :, None, :]
    wo_im = -(c_re[:, None] * q_im[:, :, None, :] + c_im[:, None] * q_re[:, :, None, :])
    wout = jnp.concatenate([wo_re, wo_im], axis=-1).transpose(0, 3, 1, 2).reshape(g, 2 * p, chunk * c)

    nk = max(1, int(math.log2(rows))) if rows > 1 else 1
    sc_re, sc_im = power(chunk * (2 ** jnp.arange(nk)))
    apow = jnp.stack([jnp.concatenate([sc_re, sc_re], -1), jnp.concatenate([-sc_im, sc_im], -1)], axis=2)
    return km.astype(BF), wst.astype(BF), wout.astype(BF), apow


def s5_mix(z, h0_re, h0_im, ops, *, width, group_ch, chunk):
    n, l, _ = z.shape
    g = width // group_ch
    rows = l // chunk
    km, wst, wout, apow = ops
    u = z[..., :width].reshape(n, rows, chunk, g, group_ch)
    u_g = u.transpose(3, 0, 1, 2, 4).reshape(g, n * rows, chunk * group_ch)
    h0 = jnp.concatenate([h0_re, h0_im], axis=-1).transpose(1, 0, 2)
    y_g, hend = s5_core(u_g, km, wst, wout, apow, h0, nseq=n, rows=rows)
    y = y_g.reshape(g, n, rows, chunk, group_ch).transpose(1, 2, 3, 0, 4).reshape(n * l, width)
    p = h0_re.shape[-1]
    hend = hend.transpose(1, 0, 2)
    return y, hend[..., :p], hend[..., p:]


def _gelu_tanh(x):
    return 0.5 * x * (1.0 + jnp.tanh(math.sqrt(2.0 / math.pi) * (x + 0.044715 * (x * x * x))))


def _tail_body(*refs, glu, final_norm, width):
    it = iter(refs)
    h_ref, ym_ref, ymem_ref, wo_ref = next(it), next(it), next(it), next(it)
    if glu:
        wg_ref, bg_ref = next(it), next(it)
    gm_ref, wu_ref, wd_ref = next(it), next(it), next(it)
    if final_norm:
        gf_ref = next(it)
    o_ref, xn_scr = next(it), next(it)
    c = pl.program_id(1)

    @pl.when(c == 0)
    def _():
        ym = ym_ref[...]
        if glu:
            zz = _gelu_tanh(ym)
            gate = jax.nn.sigmoid(_dot(zz.astype(BF), wg_ref[...]) + bg_ref[...])
            ym = zz * gate
        d = wo_ref.shape[0]
        h = (h_ref[...] + _dot(ym.astype(BF), wo_ref[0:width, :])
             + _dot(ymem_ref[...].astype(BF), wo_ref[width:d, :]))
        xn_scr[...] = _rms(h, gm_ref[...]).astype(BF)
        o_ref[...] = h

    up = jnp.maximum(_dot(xn_scr[...], wu_ref[...]), 0.0)
    o_ref[...] += _dot((up * up).astype(BF), wd_ref[...])

    if final_norm:
        @pl.when(c == pl.num_programs(1) - 1)
        def _():
            o_ref[...] = _rms(o_ref[...], gf_ref[...])


def layer_tail(h, y_main, y_mem, w_out, g_mlp, w_up, w_down, *, glu=None, final_g=None, tm=512, ff_chunk=1024):
    t, d = h.shape
    width = y_main.shape[1]
    wm = y_mem.shape[1]
    dff = w_up.shape[1]
    tm = min(tm, t)
    ff_chunk = min(ff_chunk, dff)
    row = lambda i, c: (i, 0)
    fix = lambda i, c: (0, 0)
    args = [h, y_main, y_mem, w_out]
    specs = [pl.BlockSpec((tm, d), row), pl.BlockSpec((tm, width), row), pl.BlockSpec((tm, wm), row),
             pl.BlockSpec((d, d), fix)]
    if glu is not None:
        args += [glu[0], glu[1].reshape(1, width)]
        specs += [pl.BlockSpec((width, width), fix), pl.BlockSpec((1, width), fix)]
    args += [g_mlp.reshape(1, d), w_up, w_down]
    specs += [pl.BlockSpec((1, d), fix), pl.BlockSpec((d, ff_chunk), lambda i, c: (0, c)),
              pl.BlockSpec((ff_chunk, d), lambda i, c: (c, 0))]
    if final_g is not None:
        args.append(final_g.reshape(1, d))
        specs.append(pl.BlockSpec((1, d), fix))
    body = functools.partial(_tail_body, glu=glu is not None, final_norm=final_g is not None, width=width)
    return pl.pallas_call(
        body,
        grid=(t // tm, dff // ff_chunk),
        in_specs=specs,
        out_specs=pl.BlockSpec((tm, d), row),
        out_shape=jax.ShapeDtypeStruct((t, d), F32),
        scratch_shapes=[pltpu.VMEM((tm, d), BF)],
        compiler_params=_params("parallel", "arbitrary"),
        name="layer_tail",
    )(*args)


def _log_sigmoid(x):
    return jnp.minimum(x, 0.0) - jnp.log(1.0 + jnp.exp(-jnp.abs(x)))


def _kvf_body(x_ref, g_ref, wkv_ref, wf_ref, bf_ref, kv_ref, kvb_ref, lf_ref, cum_ref, carry_ref,
              *, seq_len, tm):
    i = pl.program_id(1)
    hn = _rms(x_ref[...], g_ref[...]).astype(BF)
    kv = _dot_nt(wkv_ref[...], hn)
    kv_ref[0] = kv
    kvb_ref[0] = kv.astype(BF)
    lf = _log_sigmoid(_dot_nt(wf_ref[...], hn) + bf_ref[...])
    lf_ref[0] = lf
    src = lax.broadcasted_iota(jnp.int32, (tm, tm), 0)
    dst = lax.broadcasted_iota(jnp.int32, (tm, tm), 1)
    keep = src <= dst
    if seq_len < tm:
        keep = keep & ((src // seq_len) == (dst // seq_len))
    tri = jnp.where(keep, 1.0, 0.0).astype(BF)
    hi, mid, lo = _split3(lf)
    cum = _dot(hi, tri) + _dot(mid, tri) + _dot(lo, tri)
    if seq_len > tm:
        @pl.when(i % (seq_len // tm) == 0)
        def _():
            carry_ref[...] = jnp.zeros(carry_ref.shape, F32)
        cum = cum + carry_ref[:, 0:1]
        carry_ref[...] = jnp.broadcast_to(cum[:, tm - 1:tm], carry_ref.shape)
    cum_ref[0] = cum


def shared_kv_t(h, g, wkv_t, wf_t, bf_col, *, nb, seq_len, tm=512):
    t, d = h.shape
    lb = t // nb
    tm = min(tm, lb)
    f2 = wkv_t.shape[0]
    tiles = lb // tm
    body = functools.partial(_kvf_body, seq_len=seq_len, tm=tm)
    return pl.pallas_call(
        body,
        grid=(nb, tiles),
        in_specs=[pl.BlockSpec((tm, d), lambda b, i: (b * tiles + i, 0)),
                  pl.BlockSpec((1, d), lambda b, i: (0, 0)),
                  pl.BlockSpec((f2, d), lambda b, i: (0, 0)),
                  pl.BlockSpec((HEAD_PAD, d), lambda b, i: (0, 0)),
                  pl.BlockSpec((HEAD_PAD, 1), lambda b, i: (0, 0))],
        out_specs=[pl.BlockSpec((1, f2, tm), lambda b, i: (b, 0, i)),
                   pl.BlockSpec((1, f2, tm), lambda b, i: (b, 0, i)),
                   pl.BlockSpec((1, HEAD_PAD, tm), lambda b, i: (b, 0, i)),
                   pl.BlockSpec((1, HEAD_PAD, tm), lambda b, i: (b, 0, i))],
        out_shape=[jax.ShapeDtypeStruct((nb, f2, lb), F32),
                   jax.ShapeDtypeStruct((nb, f2, lb), BF),
                   jax.ShapeDtypeStruct((nb, HEAD_PAD, lb), F32),
                   jax.ShapeDtypeStruct((nb, HEAD_PAD, lb), F32)],
        scratch_shapes=[pltpu.VMEM((HEAD_PAD, LANES), F32)],
        compiler_params=_params("parallel", "arbitrary"),
        name="shared_kv",
    )(h, g.reshape(1, d), wkv_t, wf_t, bf_col)


def _fox_body(q_ref, aq_ref, kt_ref, ak_ref, vt_ref, o_ref, m_scr, l_scr, acc_scr, *, tq, dh):
    i = pl.program_id(2)
    qcat = jnp.concatenate([(q_ref[0] * (dh ** -0.5)).astype(BF), aq_ref[0, 0]], axis=-1)
    lane = lax.broadcasted_iota(jnp.int32, qcat.shape, 1)
    olane = lax.broadcasted_iota(jnp.int32, (tq, LANES), 1)
    rowi = lax.broadcasted_iota(jnp.int32, (tq, tq), 0)
    coli = lax.broadcasted_iota(jnp.int32, (tq, tq), 1)
    out = jnp.zeros((tq, LANES), F32)
    for e in range(LANES // dh):
        sel = ((lane >= e * dh) & (lane < (e + 1) * dh)) | (
            (lane >= LANES + e * SUBLANES) & (lane < LANES + (e + 1) * SUBLANES))
        qe = jnp.where(sel, qcat, jnp.zeros_like(qcat))
        q_main = qe[:, :LANES]
        q_aug = qe[:, LANES:]
        m_scr[...] = jnp.full(m_scr.shape, NEG_INF, F32)
        l_scr[...] = jnp.zeros(l_scr.shape, F32)
        acc_scr[...] = jnp.zeros(acc_scr.shape, F32)

        def block(j, masked):
            ks = pl.ds(pl.multiple_of(j * tq, tq), tq)
            s = _dot(q_main, kt_ref[0, :, ks]) + _dot(q_aug, ak_ref[0, 0, :, ks])
            if masked:
                s = jnp.where(coli <= rowi, s, NEG_INF)
            m_prev = m_scr[...]
            m_new = jnp.maximum(m_prev, jnp.max(s, axis=-1, keepdims=True))
            alpha = jnp.exp(m_prev - m_new)
            p = jnp.exp(s - m_new)
            l_scr[...] = alpha * l_scr[...] + jnp.sum(p, axis=-1, keepdims=True)
            acc_scr[...] = alpha * acc_scr[...] + _dot_nt(p.astype(BF), vt_ref[0, :, ks])
            m_scr[...] = m_new

        def loop_body(j, carry):
            block(j, False)
            return carry

        lax.fori_loop(0, i, loop_body, 0)
        block(i, True)
        o = acc_scr[...] / l_scr[...]
        out = jnp.where((olane >= e * dh) & (olane < (e + 1) * dh), o, out)
    o_ref[0] = out


def fox_prompt(z, aug_q, kt_b, aug_k, vt_b, *, dh, width, tq=512):
    n, l, _ = z.shape
    nblk = width // LANES
    tq = min(tq, l)
    body = functools.partial(_fox_body, tq=tq, dh=dh)
    return pl.pallas_call(
        body,
        grid=(n, nblk, l // tq),
        in_specs=[pl.BlockSpec((1, tq, LANES), lambda b, j, i: (b, i, j)),
                  pl.BlockSpec((1, 1, tq, LANES), lambda b, j, i: (b, j, i, 0)),
                  pl.BlockSpec((1, LANES, l), lambda b, j, i: (b, j, 0)),
                  pl.BlockSpec((1, 1, LANES, l), lambda b, j, i: (b, j, 0, 0)),
                  pl.BlockSpec((1, LANES, l), lambda b, j, i: (b, nblk + j, 0))],
        out_specs=pl.BlockSpec((1, tq, LANES), lambda b, j, i: (b, i, j)),
        out_shape=jax.ShapeDtypeStruct((n, l, width), F32),
        scratch_shapes=[pltpu.VMEM((tq, 1), F32), pltpu.VMEM((tq, 1), F32), pltpu.VMEM((tq, LANES), F32)],
        compiler_params=_params("parallel", "parallel", "arbitrary"),
        name="fox_prompt",
    )(z, aug_q, kt_b, aug_k, vt_b)


def fox_bias_lanes(f_t, *, dh):
    n, h, l = f_t.shape
    per = LANES // dh

    def chop(x):
        return lax.bitcast_convert_type(lax.bitcast_convert_type(x, jnp.uint32) & jnp.uint32(0xFFFF0000), F32)

    hi = chop(f_t)
    mid = chop(f_t - hi)
    lo = (f_t - hi - mid).astype(BF)
    hi = hi.astype(BF)
    mid = mid.astype(BF)
    one = jnp.ones_like(hi)
    zero = jnp.zeros_like(hi)
    qa = jnp.stack([hi, mid, lo, one, one, one, zero, zero], axis=2)
    ka = jnp.stack([one, one, one, -hi, -mid, -lo, zero, zero], axis=2)
    pad = LANES - per * SUBLANES
    qa = qa.reshape(n, h // per, per * SUBLANES, l)
    ka = ka.reshape(n, h // per, per * SUBLANES, l)
    qa = jnp.pad(qa, ((0, 0), (0, 0), (0, pad), (0, 0))).transpose(0, 1, 3, 2)
    ka = jnp.pad(ka, ((0, 0), (0, 0), (0, pad), (0, 0)))
    return qa, ka


def _past_decay_body(pt_ref, *refs, gp):
    pages = refs[:gp]
    o_ref = refs[gp]
    carry_ref = refs[gp + 1]
    j = pl.program_id(1)

    @pl.when(j == 0)
    def _():
        carry_ref[...] = jnp.zeros(carry_ref.shape, F32)

    src = lax.broadcasted_iota(jnp.int32, (LANES, 2 * LANES), 0)
    dst = lax.broadcasted_iota(jnp.int32, (LANES, 2 * LANES), 1)
    w = jnp.where((dst >= LANES) | (src > dst), 1.0, 0.0).astype(BF)
    carry = carry_ref[...]
    for i in reversed(range(gp)):
        hi, mid, lo = _split3(pages[i][0])
        r = _dot(hi, w) + _dot(mid, w) + _dot(lo, w)
        o_ref[0, :, i * LANES:(i + 1) * LANES] = r[:, :LANES] + carry
        carry = carry + r[:, LANES:]
    carry_ref[...] = carry


def past_decay(page_table, logf_pages):
    nb, npg = page_table.shape
    page = logf_pages.shape[-1]
    gp = min(LOGF_PAGES_PER_STEP, npg)
    steps = npg // gp

    def page_spec(i):
        return pl.BlockSpec((1, HEAD_PAD, page),
                            lambda b, j, pt: (pt[b, (steps - 1 - j) * gp + i], 0, 0))

    grid_spec = pltpu.PrefetchScalarGridSpec(
        num_scalar_prefetch=1,
        grid=(nb, steps),
        in_specs=[page_spec(i) for i in range(gp)],
        out_specs=pl.BlockSpec((1, HEAD_PAD, gp * page), lambda b, j, pt: (b, 0, steps - 1 - j)),
        scratch_shapes=[pltpu.VMEM((HEAD_PAD, LANES), F32)],
    )
    return pl.pallas_call(
        functools.partial(_past_decay_body, gp=gp),
        grid_spec=grid_spec,
        out_shape=jax.ShapeDtypeStruct((nb, HEAD_PAD, npg * page), F32),
        compiler_params=_params("parallel", "arbitrary"),
        name="past_decay",
    )(page_table, *([logf_pages] * gp))


def _decode_body(pt_ref, qbd_ref, r_ref, kn_ref, vn_ref, bn_ref, *refs, g, heads, lq):
    k_pages = refs[:g]
    v_pages = refs[g:2 * g]
    o_ref = refs[2 * g]
    m_scr, l_scr, acc_scr = refs[2 * g + 1:]
    j = pl.program_id(1)
    qbd = qbd_ref[0]

    @pl.when(j == 0)
    def _():
        s = _dot(qbd, kn_ref[0]) + bn_ref[0]
        m = jnp.max(s, axis=-1, keepdims=True)
        p = jnp.exp(s - m)
        m_scr[...] = m
        l_scr[...] = jnp.sum(p, axis=-1, keepdims=True)
        acc_scr[...] = _dot_nt(p.astype(BF), vn_ref[0])

    s = jnp.concatenate([_dot(qbd, k_pages[i][0].astype(BF)) for i in range(g)], axis=-1)
    rows, cols = s.shape
    s = (s.reshape(heads, lq, cols) + r_ref[0][:heads, None, :]).reshape(rows, cols)
    m_prev = m_scr[...]
    m_new = jnp.maximum(m_prev, jnp.max(s, axis=-1, keepdims=True))
    alpha = jnp.exp(m_prev - m_new)
    p = jnp.exp(s - m_new).astype(BF)
    l_scr[...] = alpha * l_scr[...] + jnp.sum(p.astype(F32), axis=-1, keepdims=True)
    acc = alpha * acc_scr[...]
    page = cols // g
    for i in range(g):
        acc = acc + _dot_nt(p[:, i * page:(i + 1) * page], v_pages[i][0].astype(BF))
    acc_scr[...] = acc
    m_scr[...] = m_new

    @pl.when(j == pl.num_programs(1) - 1)
    def _():
        o_ref[0] = acc_scr[...] / l_scr[...]


def fox_decode(page_table, qbd, decay, kn_t, vn_t, bias_new, k_pages_t, v_pages_t, *, heads, lq):
    nb, npg = page_table.shape
    rows, width = qbd.shape[1:]
    page = k_pages_t.shape[-1]
    g = min(DEC_PAGES_PER_STEP, npg)

    def page_spec(i):
        return pl.BlockSpec((1, width, page), lambda b, j, pt: (pt[b, j * g + i], 0, 0))

    grid_spec = pltpu.PrefetchScalarGridSpec(
        num_scalar_prefetch=1,
        grid=(nb, npg // g),
        in_specs=[pl.BlockSpec((1, rows, width), lambda b, j, pt: (b, 0, 0)),
                  pl.BlockSpec((1, HEAD_PAD, g * page), lambda b, j, pt: (b, 0, j)),
                  pl.BlockSpec((1, width, LANES), lambda b, j, pt: (b, 0, 0)),
                  pl.BlockSpec((1, width, LANES), lambda b, j, pt: (b, 0, 0)),
                  pl.BlockSpec((1, rows, LANES), lambda b, j, pt: (b, 0, 0))]
                 + [page_spec(i) for i in range(g)] + [page_spec(i) for i in range(g)],
        out_specs=pl.BlockSpec((1, rows, width), lambda b, j, pt: (b, 0, 0)),
        scratch_shapes=[pltpu.VMEM((rows, 1), F32), pltpu.VMEM((rows, 1), F32),
                        pltpu.VMEM((rows, width), F32)],
    )
    return pl.pallas_call(
        functools.partial(_decode_body, g=g, heads=heads, lq=lq),
        grid_spec=grid_spec,
        out_shape=jax.ShapeDtypeStruct((nb, rows, width), F32),
        compiler_params=_params("parallel", "arbitrary"),
        name="fox_decode",
    )(page_table, qbd, decay, kn_t, vn_t, bias_new, *([k_pages_t] * g), *([v_pages_t] * g))


def kernel(x_prompt, x_sample, state_s5_re, state_s5_im, cache_mem_k, cache_mem_v, cache_k, cache_v, cache_logf, page_table, mem_prompt, norm_mix, norm_mlp, w_in, w_out, w_up, w_down, w_mem_kv, s5_a_re, s5_a_im, s5_log_step, s5_b_re, s5_b_im, s5_c_re, s5_c_im, s5_d, s5_w_glu, s5_b_glu, norm_kv, w_kv, w_f, b_f, norm_final):
    n_p, seq, d_model = x_prompt.shape
    n_s, dec_seq, _ = x_sample.shape
    depth = w_in.shape[0]
    n_a = s5_a_re.shape[0]
    assert depth == 2 and n_a == 1, "layer pattern: one S5 layer then one FoX layer"
    mem_heads, mem_dh = cache_mem_k.shape[3:]
    d_mem = mem_heads * mem_dh
    width = d_model - d_mem
    fox_heads, fox_dh = cache_k.shape[2:]
    page = cache_k.shape[1]
    group_ch = s5_b_re.shape[-1]
    mem_col = width // d_mem
    assert width % d_mem == 0 and width % LANES == 0 and LANES % fox_dh == 0
    assert fox_heads <= HEAD_PAD and dec_seq % SUBLANES == 0 and page == LANES

    w_in_b = w_in.astype(BF)
    w_out_b = w_out.astype(BF)
    w_up_b = w_up.astype(BF)
    w_down_b = w_down.astype(BF)
    w_glu_b = s5_w_glu.astype(BF)
    w_memkv_t = w_mem_kv.transpose(0, 2, 1).astype(BF)
    wkv_t = w_kv.T.astype(BF)
    wf_t = jnp.pad(w_f.T, ((0, HEAD_PAD - fox_heads), (0, 0))).astype(BF)
    bf_col = jnp.pad(b_f, (0, HEAD_PAD - fox_heads)).reshape(HEAD_PAD, 1)

    def mem_t(c):
        return c.transpose(0, 1, 3, 4, 2).reshape(c.shape[0], c.shape[1], d_mem, c.shape[2])

    k_pages_t = cache_k.transpose(0, 2, 3, 1).reshape(cache_k.shape[0], width, page)
    v_pages_t = cache_v.transpose(0, 2, 3, 1).reshape(cache_v.shape[0], width, page)
    logf_pages = jnp.pad(cache_logf.transpose(0, 2, 1), ((0, 0), (0, HEAD_PAD - fox_heads), (0, 0)))

    def s5_ops(chunk, rows):
        return s5_operators(s5_a_re[0], s5_a_im[0], s5_log_step[0], s5_b_re[0], s5_b_im[0],
                            s5_c_re[0], s5_c_im[0], s5_d[0], chunk=chunk, rows=rows)

    def layer0(x, h0_re, h0_im, mk_t, mv_t, chunk):
        n, l, _ = x.shape
        h = x.reshape(n * l, d_model)
        z = rms_matmul(h, norm_mix[0], w_in_b[0]).reshape(n, l, d_model)
        y_mem = memory_attend(z, mk_t, mv_t, heads=mem_heads, col_block=mem_col).reshape(n * l, d_mem)
        y_s5, hr, hi = s5_mix(z, h0_re, h0_im, s5_ops(chunk, l // chunk),
                              width=width, group_ch=group_ch, chunk=chunk)
        h = layer_tail(h, y_s5, y_mem, w_out_b[0], norm_mlp[0], w_up_b[0], w_down_b[0],
                       glu=(w_glu_b[0], s5_b_glu[0]))
        return h, hr, hi

    def layer1_front(h, n, l, mk_t, mv_t, nb, seq_len):
        z = rms_matmul(h, norm_mix[1], w_in_b[1]).reshape(n, l, d_model)
        y_mem = memory_attend(z, mk_t, mv_t, heads=mem_heads, col_block=mem_col).reshape(n * l, d_mem)
        kv_t, kv_tb, lf_t, cum_t = shared_kv_t(h, norm_kv, wkv_t, wf_t, bf_col, nb=nb, seq_len=seq_len)
        return z, y_mem, kv_t, kv_tb, lf_t, cum_t

    def layer1_back(h, y_fox, y_mem):
        return layer_tail(h, y_fox, y_mem, w_out_b[1], norm_mlp[1], w_up_b[1], w_down_b[1],
                          final_g=norm_final)

    memkv_t = memory_kv_t(mem_prompt, w_memkv_t)
    m_tok = mem_prompt.shape[1]
    pmk_t = memkv_t[:, :, :d_mem]
    pmv_t = memkv_t[:, :, d_mem:]
    to_mem = lambda a: a.reshape(depth, n_p, mem_heads, mem_dh, m_tok).transpose(0, 1, 4, 2, 3)
    p_mem_k = to_mem(pmk_t)
    p_mem_v = to_mem(pmv_t)

    zeros = jnp.zeros((n_p, width // group_ch, s5_a_re.shape[-1]), F32)
    chunk_p = S5_CHUNK if seq % S5_CHUNK == 0 else seq
    h, p_hr, p_hi = layer0(x_prompt, zeros, zeros, pmk_t[0], pmv_t[0], chunk_p)
    z, y_mem, kv_t, kv_tb, lf_t, cum_t = layer1_front(h, n_p, seq, pmk_t[1], pmv_t[1], n_p, seq)
    aug_q, aug_k = fox_bias_lanes(cum_t[:, :fox_heads], dh=fox_dh)
    y_fox = fox_prompt(z, aug_q, kv_tb, aug_k, kv_tb, dh=fox_dh, width=width).reshape(n_p * seq, width)
    y_prompt = layer1_back(h, y_fox, y_mem).reshape(n_p, seq, d_model)
    to_heads = lambda a, n, l: a.reshape(n, fox_heads, fox_dh, l).transpose(0, 3, 1, 2)
    p_k = to_heads(kv_t[:, :width], n_p, seq)
    p_v = to_heads(kv_t[:, width:], n_p, seq)
    p_logf = lf_t[:, :fox_heads].transpose(0, 2, 1)

    cmk_t = mem_t(cache_mem_k)
    cmv_t = mem_t(cache_mem_v)
    hs, s_hr, s_hi = layer0(x_sample, state_s5_re[0], state_s5_im[0], cmk_t[0], cmv_t[0], dec_seq)
    zs, ys_mem, kvs_t, _, lfs_t, cums_t = layer1_front(hs, n_s, dec_seq, cmk_t[1], cmv_t[1], 1, dec_seq)
    tok = n_s * dec_seq
    kvs = kvs_t[0].reshape(2, fox_heads, fox_dh, n_s, dec_seq)
    s_k = kvs[0].transpose(2, 3, 0, 1)
    s_v = kvs[1].transpose(2, 3, 0, 1)
    s_logf = lfs_t[0, :fox_heads].reshape(fox_heads, n_s, dec_seq).transpose(1, 2, 0)

    q = zs[..., :width].reshape(n_s, dec_seq, fox_heads, fox_dh) * (fox_dh ** -0.5)
    eye = jnp.eye(fox_heads, dtype=F32)
    qbd = (q.transpose(0, 2, 1, 3)[:, :, :, None, :] * eye[None, :, None, :, None]).reshape(
        n_s, fox_heads * dec_seq, width).astype(BF)
    new_t = kvs_t[0].reshape(2, width, n_s, dec_seq).transpose(0, 2, 1, 3)
    new_t = jnp.pad(new_t, ((0, 0), (0, 0), (0, 0), (0, LANES - dec_seq))).astype(BF)
    c_new = cums_t[0, :fox_heads].reshape(fox_heads, n_s, dec_seq).transpose(1, 0, 2)
    qi = jnp.arange(dec_seq)[:, None]
    tp = jnp.arange(LANES)[None, :]
    c_pad = jnp.pad(c_new, ((0, 0), (0, 0), (0, LANES - dec_seq)))
    bias_new = jnp.where((tp <= qi)[None, None], -c_pad[:, :, None, :], NEG_INF).reshape(
        n_s, fox_heads * dec_seq, LANES)
    decay = past_decay(page_table, logf_pages)
    o_full = fox_decode(page_table, qbd, decay, new_t[0], new_t[1], bias_new, k_pages_t, v_pages_t,
                        heads=fox_heads, lq=dec_seq)
    o5 = o_full.reshape(n_s, fox_heads, dec_seq, fox_heads, fox_dh)
    ys_fox = jnp.einsum("bhqhd->bqhd", o5).reshape(tok, width)
    y_sample = layer1_back(hs, ys_fox, ys_mem).reshape(n_s, dec_seq, d_model)

    return (y_prompt, y_sample, p_hr[None], p_hi[None], p_mem_k, p_mem_v, p_k, p_v, p_logf,
            s_hr[None], s_hi[None], s_k, s_v, s_logf)
```

```python
import functools
import math

import jax
import jax.numpy as jnp
import numpy as np
from jax import lax
from jax.experimental import pallas as pl
from jax.experimental.pallas import tpu as pltpu

BF = jnp.bfloat16
F32 = jnp.float32
RMS_EPS = 1e-6
NEG_INF = -1e30
LOG2E = math.log2(math.e)
V7X_VMEM_BYTES = 64 * 1024 * 1024
VMEM_LIMIT = V7X_VMEM_BYTES - 8 * 1024 * 1024
LANES = 128
SUBLANES = 8
S5_CHUNK = SUBLANES
S5_ROWS_PER_TILE = 256
HEAD_PAD = 16
DEC_PAGES_PER_STEP = 8
LOGF_PAGES_PER_STEP = 16
FOX_ROWS = 256

NT_DIMS = (((1,), (1,)), ((), ()))


def _params(*sem):
    return pltpu.CompilerParams(dimension_semantics=sem, vmem_limit_bytes=VMEM_LIMIT)


def _dot(a, b):
    return jnp.dot(a, b, preferred_element_type=F32)


def _dot_nt(a, b):
    return lax.dot_general(a, b, NT_DIMS, preferred_element_type=F32)


def _rms(x, g):
    ms = jnp.mean(x * x, axis=-1, keepdims=True)
    return x * lax.rsqrt(ms + RMS_EPS) * g


def _rms_matmul_body(x_ref, g_ref, w_ref, o_ref):
    xn = _rms(x_ref[...], g_ref[...])
    o_ref[...] = _dot(xn.astype(BF), w_ref[...])


def rms_matmul(x, g, w, *, tm=512):
    t, d = x.shape
    n = w.shape[1]
    tm = min(tm, t)
    return pl.pallas_call(
        _rms_matmul_body,
        grid=(t // tm,),
        in_specs=[pl.BlockSpec((tm, d), lambda i: (i, 0)),
                  pl.BlockSpec((1, d), lambda i: (0, 0)),
                  pl.BlockSpec((d, n), lambda i: (0, 0))],
        out_specs=pl.BlockSpec((tm, n), lambda i: (i, 0)),
        out_shape=jax.ShapeDtypeStruct((t, n), F32),
        compiler_params=_params("parallel"),
        name="rms_matmul",
    )(x, g.reshape(1, d), w)


def _memkv_body(w_ref, m_ref, o_ref):
    o_ref[0, 0] = _dot_nt(w_ref[0], m_ref[0].astype(BF))


def memory_kv_t(mem, w_t):
    nb, m, d = mem.shape
    depth, f, _ = w_t.shape
    return pl.pallas_call(
        _memkv_body,
        grid=(depth, nb),
        in_specs=[pl.BlockSpec((1, f, d), lambda l, b: (l, 0, 0)),
                  pl.BlockSpec((1, m, d), lambda l, b: (b, 0, 0))],
        out_specs=pl.BlockSpec((1, 1, f, m), lambda l, b: (l, b, 0, 0)),
        out_shape=jax.ShapeDtypeStruct((depth, nb, f, m), F32),
        compiler_params=_params("parallel", "parallel"),
        name="memory_kv",
    )(w_t, mem)


def _memattn_body(q_ref, kt_ref, vt_ref, o_ref, *, heads, dh):
    q = q_ref[0] * (dh ** -0.5)
    kt = kt_ref[0].astype(BF)
    vt = vt_ref[0].astype(BF)
    lane = lax.broadcasted_iota(jnp.int32, q.shape, 1)
    out = jnp.zeros(q.shape, F32)
    for h in range(heads):
        sel = (lane >= h * dh) & (lane < (h + 1) * dh)
        s = _dot(jnp.where(sel, q, 0.0).astype(BF), kt)
        m = jnp.max(s, axis=-1, keepdims=True)
        p = jnp.exp(s - m)
        l = jnp.sum(p, axis=-1, keepdims=True)
        o = _dot_nt((p / l).astype(BF), vt)
        out = jnp.where(sel, o, out)
    o_ref[0] = out


def memory_attend(z, kt, vt, *, heads, col_block, tm=512):
    n, l, _ = z.shape
    f, m = kt.shape[1:]
    tm = min(tm, l)
    body = functools.partial(_memattn_body, heads=heads, dh=f // heads)
    return pl.pallas_call(
        body,
        grid=(n, l // tm),
        in_specs=[pl.BlockSpec((1, tm, f), lambda b, i: (b, i, col_block)),
                  pl.BlockSpec((1, f, m), lambda b, i: (b, 0, 0)),
                  pl.BlockSpec((1, f, m), lambda b, i: (b, 0, 0))],
        out_specs=pl.BlockSpec((1, tm, f), lambda b, i: (b, i, 0)),
        out_shape=jax.ShapeDtypeStruct((n, l, f), F32),
        compiler_params=_params("parallel", "parallel"),
        name="memory_attend",
    )(z, kt, vt)


def _cmul(a, h):
    return a[0:1, :] * h + a[1:2, :] * pltpu.roll(h, h.shape[-1] // 2, axis=1)


def _s5_expand(src_ref, dst_ref, *, row_inner, col_inner):
    per = src_ref.shape[1] // LANES
    wide = per * LANES
    r = lax.broadcasted_iota(jnp.int32, (LANES, wide), 0)
    q = lax.broadcasted_iota(jnp.int32, (LANES, wide), 1)
    tile = jnp.where((r // col_inner == q // (per * col_inner)) & (r % col_inner == q % col_inner),
                     1.0, 0.0).astype(BF)
    spread = _dot(src_ref[0], tile).astype(BF)
    qg = (lax.broadcasted_iota(jnp.int32, (row_inner, wide), 1) // col_inner) % per
    for g in range(per):
        for a in range(LANES // row_inner):
            blk = spread[g * LANES + a * row_inner:g * LANES + (a + 1) * row_inner, :]
            dst_ref[pl.ds((a * per + g) * row_inner, row_inner), :] = jnp.where(qg == g, blk, jnp.zeros_like(blk))


def _s5_body(z_ref, kc_ref, wc_ref, oc_ref, apow_ref, h0_ref, y_ref, hend_ref, carry_ref,
             kx_scr, wst_scr, wout_scr, *, rt, per_row_state, group_ch, state):
    i = pl.program_id(2)
    t_len = S5_CHUNK

    @pl.when((pl.program_id(1) == 0) & (i == 0))
    def _():
        _s5_expand(kc_ref, kx_scr, row_inner=group_ch, col_inner=group_ch)
        _s5_expand(wc_ref, wst_scr, row_inner=group_ch, col_inner=state)
        _s5_expand(oc_ref, wout_scr, row_inner=state, col_inner=group_ch)

    lhs = jnp.concatenate([z_ref[0, pl.ds(t, rt, stride=t_len), :].astype(BF) for t in range(t_len)],
                          axis=-1)
    y = _dot(lhs, kx_scr[...])
    hs = _dot(lhs, wst_scr[...])
    a_chunk = apow_ref[0, 0]
    if per_row_state:
        h_in = h0_ref[0, 0]
        hs = hs + _cmul(a_chunk, h_in)
        hend_ref[0, 0] = hs
    else:
        @pl.when(i == 0)
        def _():
            carry_ref[...] = h0_ref[0, 0]
        h0 = carry_ref[...]
        rid = lax.broadcasted_iota(jnp.int32, hs.shape, 0)
        hs = hs + jnp.where(rid == 0, _cmul(a_chunk, h0), 0.0)
        step, k = 1, 0
        while step < rt:
            shifted = jnp.where(rid >= step, pltpu.roll(hs, step, axis=0), 0.0)
            hs = hs + _cmul(apow_ref[0, k], shifted)
            step *= 2
            k += 1
        h_in = jnp.where(rid == 0, h0, pltpu.roll(hs, 1, axis=0))
        carry_ref[...] = hs[rt - 1:rt, :]
        hend_ref[0, 0] = hs[rt - 1:rt, :]
    y = y + _dot(h_in.astype(BF), wout_scr[...])
    for t in range(t_len):
        y_ref[0, pl.ds(t, rt, stride=t_len), :] = y[:, t * LANES:(t + 1) * LANES]


def s5_core(z, ops, h0, *, width, per_row_state):
    kc, wc, oc, apow = ops
    nb, l, _ = z.shape
    nblk = width // LANES
    sw = kc.shape[1]
    state = sw // 2 // (sw // LANES)
    rows_total = l // S5_CHUNK
    rt = rows_total if per_row_state else min(S5_ROWS_PER_TILE, rows_total)
    tiles = rows_total // rt
    rs = h0.shape[2]
    nk = apow.shape[1]
    body = functools.partial(_s5_body, rt=rt, per_row_state=per_row_state,
                             group_ch=LANES // (sw // LANES), state=state)
    compact = pl.BlockSpec((1, sw, LANES), lambda j, b, i: (j, 0, 0))
    return pl.pallas_call(
        body,
        grid=(nblk, nb, tiles),
        in_specs=[pl.BlockSpec((1, rt * S5_CHUNK, LANES), lambda j, b, i: (b, i, j)),
                  compact, compact, compact,
                  pl.BlockSpec((1, nk, 2, sw), lambda j, b, i: (j, 0, 0, 0)),
                  pl.BlockSpec((1, 1, rs, sw), lambda j, b, i: (j, b, 0, 0))],
        out_specs=[pl.BlockSpec((1, rt * S5_CHUNK, LANES), lambda j, b, i: (b, i, j)),
                   pl.BlockSpec((1, 1, rs, sw), lambda j, b, i: (j, b, 0, 0))],
        out_shape=[jax.ShapeDtypeStruct((nb, l, width), F32),
                   jax.ShapeDtypeStruct((nblk, nb, rs, sw), F32)],
        scratch_shapes=[pltpu.VMEM((1, sw), F32), pltpu.VMEM((sw, sw), BF), pltpu.VMEM((sw, sw), BF),
                        pltpu.VMEM((sw, sw), BF)],
        compiler_params=_params("arbitrary", "arbitrary", "arbitrary"),
        name="s5_core",
    )(z, kc, wc, oc, apow, h0)


def s5_operators(a_re, a_im, log_step, b_re, b_im, c_re, c_im, d):
    hp = lax.Precision.HIGHEST
    chunk = S5_CHUNK
    g, p = a_re.shape
    c = b_re.shape[-1]
    per = LANES // c
    nblk = g // per
    dt = jnp.exp(log_step.astype(F32))[:, None]
    a_re = a_re.astype(F32)
    a_im = a_im.astype(F32)
    lam_re = a_re * dt
    lam_im = a_im * dt
    mag = jnp.exp(lam_re)
    ab_re = mag * jnp.cos(lam_im)
    ab_im = mag * jnp.sin(lam_im)
    den = a_re * a_re + a_im * a_im
    nr = ab_re - 1.0
    ni = ab_im
    cr = (nr * a_re + ni * a_im) / den
    ci = (ni * a_re - nr * a_im) / den
    bb_re = cr[..., None] * b_re - ci[..., None] * b_im
    bb_im = cr[..., None] * b_im + ci[..., None] * b_re

    def power(tau):
        tau = tau.astype(F32)[None, :, None]
        m = jnp.exp(lam_re[:, None, :] * tau)
        ph = lam_im[:, None, :] * tau
        return m * jnp.cos(ph), m * jnp.sin(ph)

    pw_re, pw_im = power(jnp.arange(chunk + 1))
    e_re = pw_re[..., None] * bb_re[:, None] - pw_im[..., None] * bb_im[:, None]
    e_im = pw_re[..., None] * bb_im[:, None] + pw_im[..., None] * bb_re[:, None]
    ktau = (jnp.einsum("gop,gtpi->gtoi", c_re, e_re[:, :chunk], precision=hp)
            - jnp.einsum("gop,gtpi->gtoi", c_im, e_im[:, :chunk], precision=hp))
    ktau = ktau.at[:, 0].add(d[:, :, None] * jnp.eye(c, dtype=F32)[None])
    s_idx = jnp.arange(chunk)[:, None]
    t_idx = jnp.arange(chunk)[None, :]
    lag = t_idx - s_idx
    toe = jnp.where((lag >= 0)[None, :, :, None, None], ktau[:, jnp.maximum(lag, 0)], 0.0)
    kc = toe.transpose(0, 1, 4, 2, 3).reshape(nblk, per * chunk * c, chunk * c)

    rev = chunk - 1 - jnp.arange(chunk)
    w_c = jnp.stack([e_re[:, rev], e_im[:, rev]], axis=2)
    wc = w_c.transpose(0, 1, 4, 2, 3).reshape(nblk, per * chunk * c, 2 * p)

    q_re = pw_re[:, 1:chunk + 1]
    q_im = pw_im[:, 1:chunk + 1]
    wo_re = c_re[:, None] * q_re[:, :, None, :] - c_im[:, None] * q_im[:, :, None, :]
    wo_im = -(c_re[:, None] * q_im[:, :, None, :] + c_im[:, None] * q_re[:, :, None, :])
    o_c = jnp.stack([wo_re, wo_im], axis=1)
    oc = o_c.transpose(0, 1, 4, 2, 3).reshape(nblk, per * 2 * p, chunk * c)

    nk = max(1, int(math.log2(S5_ROWS_PER_TILE)))
    sc_re, sc_im = power(chunk * (2 ** jnp.arange(nk)))
    lay = lambda x: x.reshape(nblk, per, nk, p).transpose(0, 2, 1, 3).reshape(nblk, nk, per * p)
    sc_re, sc_im = lay(sc_re), lay(sc_im)
    apow = jnp.stack([jnp.concatenate([sc_re, sc_re], -1), jnp.concatenate([-sc_im, sc_im], -1)], axis=2)
    return kc.astype(BF), wc.astype(BF), oc.astype(BF), apow


def s5_state_in(h_re, h_im, per):
    n, g, p = h_re.shape
    f = lambda x: x.reshape(n, g // per, per * p)
    return jnp.concatenate([f(h_re), f(h_im)], axis=-1).transpose(1, 0, 2)


def s5_state_out(h, per, p):
    nblk, n, w2 = h.shape
    h = h.transpose(1, 0, 2)
    f = lambda x: x.reshape(n, nblk * per, p)
    return f(h[..., :w2 // 2]), f(h[..., w2 // 2:])


def _gelu_tanh(x):
    return 0.5 * x * (1.0 + jnp.tanh(math.sqrt(2.0 / math.pi) * (x + 0.044715 * (x * x * x))))


def _tail_body(*refs, glu, final_norm, width):
    it = iter(refs)
    h_ref, ym_ref, ymem_ref, wo_ref = next(it), next(it), next(it), next(it)
    if glu:
        wg_ref, bg_ref = next(it), next(it)
    gm_ref, wu_ref, wd_ref = next(it), next(it), next(it)
    if final_norm:
        gf_ref = next(it)
    o_ref, xn_scr = next(it), next(it)
    c = pl.program_id(1)

    @pl.when(c == 0)
    def _():
        ym = ym_ref[...]
        if glu:
            zz = _gelu_tanh(ym)
            gate = jax.nn.sigmoid(_dot(zz.astype(BF), wg_ref[...]) + bg_ref[...])
            ym = zz * gate
        d = wo_ref.shape[0]
        h = (h_ref[...] + _dot(ym.astype(BF), wo_ref[0:width, :])
             + _dot(ymem_ref[...].astype(BF), wo_ref[width:d, :]))
        xn_scr[...] = _rms(h, gm_ref[...]).astype(BF)
        o_ref[...] = h

    up = jnp.maximum(_dot(xn_scr[...], wu_ref[...]), 0.0)
    o_ref[...] += _dot((up * up).astype(BF), wd_ref[...])

    if final_norm:
        @pl.when(c == pl.num_programs(1) - 1)
        def _():
            o_ref[...] = _rms(o_ref[...], gf_ref[...])


def layer_tail(h, y_main, y_mem, w_out, g_mlp, w_up, w_down, *, glu=None, final_g=None, tm=512, ff_chunk=1024):
    t, d = h.shape
    width = y_main.shape[1]
    wm = y_mem.shape[1]
    dff = w_up.shape[1]
    tm = min(tm, t)
    ff_chunk = min(ff_chunk, dff)
    row = lambda i, c: (i, 0)
    fix = lambda i, c: (0, 0)
    args = [h, y_main, y_mem, w_out]
    specs = [pl.BlockSpec((tm, d), row), pl.BlockSpec((tm, width), row), pl.BlockSpec((tm, wm), row),
             pl.BlockSpec((d, d), fix)]
    if glu is not None:
        args += [glu[0], glu[1].reshape(1, width)]
        specs += [pl.BlockSpec((width, width), fix), pl.BlockSpec((1, width), fix)]
    args += [g_mlp.reshape(1, d), w_up, w_down]
    specs += [pl.BlockSpec((1, d), fix), pl.BlockSpec((d, ff_chunk), lambda i, c: (0, c)),
              pl.BlockSpec((ff_chunk, d), lambda i, c: (c, 0))]
    if final_g is not None:
        args.append(final_g.reshape(1, d))
        specs.append(pl.BlockSpec((1, d), fix))
    body = functools.partial(_tail_body, glu=glu is not None, final_norm=final_g is not None, width=width)
    return pl.pallas_call(
        body,
        grid=(t // tm, dff // ff_chunk),
        in_specs=specs,
        out_specs=pl.BlockSpec((tm, d), row),
        out_shape=jax.ShapeDtypeStruct((t, d), F32),
        scratch_shapes=[pltpu.VMEM((tm, d), BF)],
        compiler_params=_params("parallel", "arbitrary"),
        name="layer_tail",
    )(*args)


def _log_sigmoid(x):
    return jnp.minimum(x, 0.0) - jnp.log(1.0 + jnp.exp(-jnp.abs(x)))


def _chop(x):
    return lax.bitcast_convert_type(lax.bitcast_convert_type(x, jnp.uint32) & jnp.uint32(0xFFFF0000), F32)


def _split3(x):
    hi = _chop(x)
    mid = _chop(x - hi)
    lo = x - hi - mid
    return hi.astype(BF), mid.astype(BF), lo.astype(BF)


def _kvf_body(x_ref, g_ref, wkvt_ref, wk_ref, wft_ref, bfc_ref, wf_ref, bfr_ref, pq_ref, cq_ref, pk_ref, ck_ref,
              kt_ref, vt_ref, vtb_ref, ktok_ref, lft_ref, cum_ref, aq_ref, ak_ref, carry_ref,
              *, seq_len, tm, width):
    i = pl.program_id(1)
    hn = _rms(x_ref[...], g_ref[...]).astype(BF)
    kv = _dot_nt(wkvt_ref[...], hn)
    kt_ref[0] = kv[:width]
    vt_ref[0] = kv[width:]
    vtb_ref[0] = kv[width:].astype(BF)
    ktok_ref[...] = _dot(hn, wk_ref[...]).astype(BF)
    lft_ref[0] = _log_sigmoid(_dot_nt(wft_ref[...], hn) + bfc_ref[...])
    lf = _log_sigmoid(_dot(hn, wf_ref[...]) + bfr_ref[...])
    dst = lax.broadcasted_iota(jnp.int32, (tm, tm), 0)
    src = lax.broadcasted_iota(jnp.int32, (tm, tm), 1)
    keep = src <= dst
    if seq_len < tm:
        keep = keep & ((src // seq_len) == (dst // seq_len))
    tri = jnp.where(keep, 1.0, 0.0).astype(BF)
    hi, mid, lo = _split3(lf)
    cum = _dot(tri, hi) + _dot(tri, mid) + _dot(tri, lo)
    if seq_len > tm:
        @pl.when(i % (seq_len // tm) == 0)
        def _():
            carry_ref[...] = jnp.zeros(carry_ref.shape, F32)
        cum = cum + carry_ref[...]
        carry_ref[...] = cum[tm - 1:tm, :]
    cum_ref[...] = cum
    parts = jnp.concatenate(_split3(cum * LOG2E), axis=-1)
    aq_ref[...] = (_dot(parts, pq_ref[...]) + cq_ref[...]).astype(BF)
    ak_ref[...] = (_dot(parts, pk_ref[...]) + ck_ref[...]).astype(BF)


def fox_bias_placement(heads, dh, width):
    per = LANES // dh
    pq = np.zeros((3 * LANES, width), np.float32)
    pk = np.zeros((3 * LANES, width), np.float32)
    cq = np.zeros((1, width), np.float32)
    ck = np.zeros((1, width), np.float32)
    for h in range(heads):
        base = (h // per) * LANES + (h % per) * SUBLANES
        for term in range(3):
            pq[term * LANES + h, base + term] = 1.0
            pk[term * LANES + h, base + 3 + term] = -1.0
        cq[0, base + 3:base + 6] = 1.0
        ck[0, base:base + 3] = 1.0
    return jnp.asarray(pq, BF), jnp.asarray(cq), jnp.asarray(pk, BF), jnp.asarray(ck)


def shared_kv_t(h, g, wkv_t, wk, wf_t, bf_col, wf, bf_row, place, *, nb, seq_len, tm=512):
    t, d = h.shape
    lb = t // nb
    tm = min(tm, lb)
    f2 = wkv_t.shape[0]
    width = f2 // 2
    tiles = lb // tm
    body = functools.partial(_kvf_body, seq_len=seq_len, tm=tm, width=width)
    fix = lambda b, i: (0, 0)
    tok = lambda b, i: (b * tiles + i, 0)
    feat = lambda b, i: (b, 0, i)
    return pl.pallas_call(
        body,
        grid=(nb, tiles),
        in_specs=[pl.BlockSpec((tm, d), tok),
                  pl.BlockSpec((1, d), fix),
                  pl.BlockSpec((f2, d), fix),
                  pl.BlockSpec((d, width), fix),
                  pl.BlockSpec((HEAD_PAD, d), fix),
                  pl.BlockSpec((HEAD_PAD, 1), fix),
                  pl.BlockSpec((d, LANES), fix),
                  pl.BlockSpec((1, LANES), fix),
                  pl.BlockSpec((3 * LANES, width), fix),
                  pl.BlockSpec((1, width), fix),
                  pl.BlockSpec((3 * LANES, width), fix),
                  pl.BlockSpec((1, width), fix)],
        out_specs=[pl.BlockSpec((1, width, tm), feat),
                   pl.BlockSpec((1, width, tm), feat),
                   pl.BlockSpec((1, width, tm), feat),
                   pl.BlockSpec((tm, width), tok),
                   pl.BlockSpec((1, HEAD_PAD, tm), feat),
                   pl.BlockSpec((tm, LANES), tok),
                   pl.BlockSpec((tm, width), tok),
                   pl.BlockSpec((tm, width), tok)],
        out_shape=[jax.ShapeDtypeStruct((nb, width, lb), F32),
                   jax.ShapeDtypeStruct((nb, width, lb), F32),
                   jax.ShapeDtypeStruct((nb, width, lb), BF),
                   jax.ShapeDtypeStruct((t, width), BF),
                   jax.ShapeDtypeStruct((nb, HEAD_PAD, lb), F32),
                   jax.ShapeDtypeStruct((t, LANES), F32),
                   jax.ShapeDtypeStruct((t, width), BF),
                   jax.ShapeDtypeStruct((t, width), BF)],
        scratch_shapes=[pltpu.VMEM((1, LANES), F32)],
        compiler_params=_params("parallel", "arbitrary"),
        name="shared_kv",
    )(h, g.reshape(1, d), wkv_t, wk, wf_t, bf_col, wf, bf_row, *place)


def _fox_body(q_ref, aq_ref, k_ref, ak_ref, vt_ref, o_ref, kcat_scr, m_scr, l_scr, acc_scr, st_scr, p_scr,
              *, tq, dh):
    i = pl.program_id(2)

    @pl.when(i == 0)
    def _():
        kcat_scr[:, :LANES] = k_ref[0]
        kcat_scr[:, LANES:] = ak_ref[0]

    qcat_t = jnp.concatenate([(q_ref[0] * (dh ** -0.5 * LOG2E)).T.astype(BF),
                              aq_ref[0].astype(F32).T.astype(BF)], axis=0)
    rows_per = min(FOX_ROWS, tq)
    nslice = tq // rows_per
    frow = lax.broadcasted_iota(jnp.int32, qcat_t.shape, 0)
    key_i = lax.broadcasted_iota(jnp.int32, (rows_per, tq), 0)
    qry_i = lax.broadcasted_iota(jnp.int32, (rows_per, tq), 1)
    feat = lax.broadcasted_iota(jnp.int32, (LANES, tq), 0)
    per = LANES // dh
    qes = []
    for e in range(per):
        sel = ((frow >= e * dh) & (frow < (e + 1) * dh)) | (
            (frow >= LANES + e * SUBLANES) & (frow < LANES + (e + 1) * SUBLANES))
        qes.append(jnp.where(sel, qcat_t, jnp.zeros_like(qcat_t)))
    m_scr[...] = jnp.full(m_scr.shape, NEG_INF, F32)
    l_scr[...] = jnp.zeros(l_scr.shape, F32)
    acc_scr[...] = jnp.zeros(acc_scr.shape, F32)

    def block_rows(j):
        return pl.ds(pl.multiple_of(j * tq, tq), tq)

    def scores(j, slot):
        kblk = kcat_scr[block_rows(j), :]
        for e in range(per):
            st_scr[slot * per + e] = _dot(kblk, qes[e])

    def attend(j, slot, masked):
        vblk = vt_ref[0, :, block_rows(j)]
        for e in range(per):
            def rows(c):
                st = st_scr[slot * per + e, c * rows_per:(c + 1) * rows_per, :]
                if masked:
                    st = jnp.where(key_i + c * rows_per <= qry_i, st, NEG_INF)
                return st
            m_prev = m_scr[e:e + 1, :]
            m_new = m_prev
            for c in range(nslice):
                m_new = jnp.maximum(m_new, jnp.max(rows(c), axis=0, keepdims=True))
            alpha = jnp.exp2(m_prev - m_new)
            lsum = jnp.zeros_like(m_new)
            for c in range(nslice):
                pc = jnp.exp2(rows(c) - m_new)
                lsum = lsum + jnp.sum(pc, axis=0, keepdims=True)
                p_scr[e, c * rows_per:(c + 1) * rows_per, :] = pc.astype(BF)
            l_scr[e:e + 1, :] = alpha * l_scr[e:e + 1, :] + lsum
            acc_scr[e] = alpha * acc_scr[e] + _dot(vblk, p_scr[e])
            m_scr[e:e + 1, :] = m_new

    scores(0, 0)

    def pair(t, carry):
        scores(2 * t + 1, 1)
        attend(2 * t, 0, False)
        scores(2 * t + 2, 0)
        attend(2 * t + 1, 1, False)
        return carry

    lax.fori_loop(0, i // 2, pair, 0)

    @pl.when(i % 2 == 0)
    def _():
        attend(i, 0, True)

    @pl.when(i % 2 == 1)
    def _():
        scores(i, 1)
        attend(i - 1, 0, False)
        attend(i, 1, True)

    out_t = jnp.zeros((LANES, tq), F32)
    for e in range(per):
        o_t = acc_scr[e] / l_scr[e:e + 1, :]
        out_t = jnp.where((feat >= e * dh) & (feat < (e + 1) * dh), o_t, out_t)
    o_ref[0] = out_t.T


def fox_prompt(z, aug_q, k_tok, aug_k, vt_b, *, dh, width, tq=512):
    n, l, _ = z.shape
    nblk = width // LANES
    tq = min(tq, l)
    body = functools.partial(_fox_body, tq=tq, dh=dh)
    qblk = pl.BlockSpec((1, tq, LANES), lambda b, j, i: (b, i, j))
    seq = pl.BlockSpec((1, l, LANES), lambda b, j, i: (b, 0, j))
    return pl.pallas_call(
        body,
        grid=(n, nblk, l // tq),
        in_specs=[qblk, qblk, seq, seq,
                  pl.BlockSpec((1, LANES, l), lambda b, j, i: (b, j, 0))],
        out_specs=qblk,
        out_shape=jax.ShapeDtypeStruct((n, l, width), F32),
        scratch_shapes=[pltpu.VMEM((l, 2 * LANES), BF), pltpu.VMEM((LANES // dh, tq), F32),
                        pltpu.VMEM((LANES // dh, tq), F32), pltpu.VMEM((LANES // dh, LANES, tq), F32),
                        pltpu.VMEM((2 * (LANES // dh), tq, tq), F32), pltpu.VMEM((LANES // dh, tq, tq), BF)],
        compiler_params=_params("parallel", "parallel", "arbitrary"),
        name="fox_prompt",
    )(z, aug_q, k_tok, aug_k, vt_b)


def _past_decay_body(pt_ref, *refs, gp):
    pages = refs[:gp]
    o_ref = refs[gp]
    carry_ref = refs[gp + 1]
    j = pl.program_id(1)

    @pl.when(j == 0)
    def _():
        carry_ref[...] = jnp.zeros(carry_ref.shape, F32)

    src = lax.broadcasted_iota(jnp.int32, (LANES, 2 * LANES), 0)
    dst = lax.broadcasted_iota(jnp.int32, (LANES, 2 * LANES), 1)
    w = jnp.where((dst >= LANES) | (src > dst), 1.0, 0.0).astype(BF)
    carry = carry_ref[...]
    for i in reversed(range(gp)):
        hi, mid, lo = _split3(pages[i][0])
        r = _dot(hi, w) + _dot(mid, w) + _dot(lo, w)
        o_ref[0, :, i * LANES:(i + 1) * LANES] = r[:, :LANES] + carry
        carry = carry + r[:, LANES:]
    carry_ref[...] = carry


def past_decay(page_table, logf_pages):
    nb, npg = page_table.shape
    page = logf_pages.shape[-1]
    gp = min(LOGF_PAGES_PER_STEP, npg)
    steps = npg // gp

    def page_spec(i):
        return pl.BlockSpec((1, HEAD_PAD, page),
                            lambda b, j, pt: (pt[b, (steps - 1 - j) * gp + i], 0, 0))

    grid_spec = pltpu.PrefetchScalarGridSpec(
        num_scalar_prefetch=1,
        grid=(nb, steps),
        in_specs=[page_spec(i) for i in range(gp)],
        out_specs=pl.BlockSpec((1, HEAD_PAD, gp * page), lambda b, j, pt: (b, 0, steps - 1 - j)),
        scratch_shapes=[pltpu.VMEM((HEAD_PAD, LANES), F32)],
    )
    return pl.pallas_call(
        functools.partial(_past_decay_body, gp=gp),
        grid_spec=grid_spec,
        out_shape=jax.ShapeDtypeStruct((nb, HEAD_PAD, npg * page), F32),
        compiler_params=_params("parallel", "arbitrary"),
        name="past_decay",
    )(page_table, *([logf_pages] * gp))


def _decode_body(pt_ref, qbd_ref, r_ref, kn_ref, vn_ref, bn_ref, *refs, g, heads, lq):
    k_pages = refs[:g]
    v_pages = refs[g:2 * g]
    o_ref = refs[2 * g]
    m_scr, l_scr, acc_scr = refs[2 * g + 1:]
    j = pl.program_id(1)
    qbd = qbd_ref[0]

    @pl.when(j == 0)
    def _():
        s = _dot(qbd, kn_ref[0]) + bn_ref[0]
        m = jnp.max(s, axis=-1, keepdims=True)
        p = jnp.exp(s - m)
        m_scr[...] = m
        l_scr[...] = jnp.sum(p, axis=-1, keepdims=True)
        acc_scr[...] = _dot_nt(p.astype(BF), vn_ref[0])

    s = jnp.concatenate([_dot(qbd, k_pages[i][0].astype(BF)) for i in range(g)], axis=-1)
    rows, cols = s.shape
    s = (s.reshape(heads, lq, cols) + r_ref[0][:heads, None, :]).reshape(rows, cols)
    m_prev = m_scr[...]
    m_new = jnp.maximum(m_prev, jnp.max(s, axis=-1, keepdims=True))
    alpha = jnp.exp(m_prev - m_new)
    p = jnp.exp(s - m_new).astype(BF)
    l_scr[...] = alpha * l_scr[...] + jnp.sum(p.astype(F32), axis=-1, keepdims=True)
    acc = alpha * acc_scr[...]
    page = cols // g
    for i in range(g):
        acc = acc + _dot_nt(p[:, i * page:(i + 1) * page], v_pages[i][0].astype(BF))
    acc_scr[...] = acc
    m_scr[...] = m_new

    @pl.when(j == pl.num_programs(1) - 1)
    def _():
        o_ref[0] = acc_scr[...] / l_scr[...]


def fox_decode(page_table, qbd, decay, kn_t, vn_t, bias_new, k_pages_t, v_pages_t, *, heads, lq):
    nb, npg = page_table.shape
    rows, width = qbd.shape[1:]
    page = k_pages_t.shape[-1]
    g = min(DEC_PAGES_PER_STEP, npg)

    def page_spec(i):
        return pl.BlockSpec((1, width, page), lambda b, j, pt: (pt[b, j * g + i], 0, 0))

    grid_spec = pltpu.PrefetchScalarGridSpec(
        num_scalar_prefetch=1,
        grid=(nb, npg // g),
        in_specs=[pl.BlockSpec((1, rows, width), lambda b, j, pt: (b, 0, 0)),
                  pl.BlockSpec((1, HEAD_PAD, g * page), lambda b, j, pt: (b, 0, j)),
                  pl.BlockSpec((1, width, LANES), lambda b, j, pt: (b, 0, 0)),
                  pl.BlockSpec((1, width, LANES), lambda b, j, pt: (b, 0, 0)),
                  pl.BlockSpec((1, rows, LANES), lambda b, j, pt: (b, 0, 0))]
                 + [page_spec(i) for i in range(g)] + [page_spec(i) for i in range(g)],
        out_specs=pl.BlockSpec((1, rows, width), lambda b, j, pt: (b, 0, 0)),
        scratch_shapes=[pltpu.VMEM((rows, 1), F32), pltpu.VMEM((rows, 1), F32),
                        pltpu.VMEM((rows, width), F32)],
    )
    return pl.pallas_call(
        functools.partial(_decode_body, g=g, heads=heads, lq=lq),
        grid_spec=grid_spec,
        out_shape=jax.ShapeDtypeStruct((nb, rows, width), F32),
        compiler_params=_params("parallel", "arbitrary"),
        name="fox_decode",
    )(page_table, qbd, decay, kn_t, vn_t, bias_new, *([k_pages_t] * g), *([v_pages_t] * g))


def kernel(x_prompt, x_sample, state_s5_re, state_s5_im, cache_mem_k, cache_mem_v, cache_k, cache_v, cache_logf, page_table, mem_prompt, norm_mix, norm_mlp, w_in, w_out, w_up, w_down, w_mem_kv, s5_a_re, s5_a_im, s5_log_step, s5_b_re, s5_b_im, s5_c_re, s5_c_im, s5_d, s5_w_glu, s5_b_glu, norm_kv, w_kv, w_f, b_f, norm_final):
    n_p, seq, d_model = x_prompt.shape
    n_s, dec_seq, _ = x_sample.shape
    depth = w_in.shape[0]
    n_a = s5_a_re.shape[0]
    assert depth == 2 and n_a == 1, "layer pattern: one S5 layer then one FoX layer"
    mem_heads, mem_dh = cache_mem_k.shape[3:]
    d_mem = mem_heads * mem_dh
    width = d_model - d_mem
    fox_heads, fox_dh = cache_k.shape[2:]
    page = cache_k.shape[1]
    group_ch = s5_b_re.shape[-1]
    s5_state = s5_a_re.shape[-1]
    per = LANES // group_ch
    mem_col = width // d_mem
    assert width % d_mem == 0 and width % LANES == 0 and LANES % fox_dh == 0 and LANES % group_ch == 0
    assert fox_heads <= HEAD_PAD and page == LANES
    assert dec_seq == S5_CHUNK and seq % S5_CHUNK == 0
    assert S5_CHUNK * group_ch == LANES and 2 * s5_state == LANES
    assert (seq // S5_CHUNK) % min(S5_ROWS_PER_TILE, seq // S5_CHUNK) == 0

    w_in_b = w_in.astype(BF)
    w_out_b = w_out.astype(BF)
    w_up_b = w_up.astype(BF)
    w_down_b = w_down.astype(BF)
    w_glu_b = s5_w_glu.astype(BF)
    w_memkv_t = w_mem_kv.transpose(0, 2, 1).astype(BF)
    wkv_t = w_kv.T.astype(BF)
    wk = w_kv[:, :width].astype(BF)
    wf_t = jnp.pad(w_f.T, ((0, HEAD_PAD - fox_heads), (0, 0))).astype(BF)
    bf_col = jnp.pad(b_f, (0, HEAD_PAD - fox_heads)).reshape(HEAD_PAD, 1)
    wf = jnp.pad(w_f, ((0, 0), (0, LANES - fox_heads))).astype(BF)
    bf_row = jnp.pad(b_f, (0, LANES - fox_heads)).reshape(1, LANES)
    place = fox_bias_placement(fox_heads, fox_dh, width)

    def mem_t(c):
        return c.transpose(0, 1, 3, 4, 2).reshape(c.shape[0], c.shape[1], d_mem, c.shape[2])

    k_pages_t = cache_k.transpose(0, 2, 3, 1).reshape(cache_k.shape[0], width, page)
    v_pages_t = cache_v.transpose(0, 2, 3, 1).reshape(cache_v.shape[0], width, page)
    logf_pages = jnp.pad(cache_logf.transpose(0, 2, 1), ((0, 0), (0, HEAD_PAD - fox_heads), (0, 0)))

    s5_ops = s5_operators(s5_a_re[0], s5_a_im[0], s5_log_step[0], s5_b_re[0], s5_b_im[0],
                          s5_c_re[0], s5_c_im[0], s5_d[0])

    def layer0(x, z_view, h0, mk_t, mv_t, per_row_state):
        n, l, _ = x.shape
        h = x.reshape(n * l, d_model)
        z = rms_matmul(h, norm_mix[0], w_in_b[0]).reshape(n, l, d_model)
        y_mem = memory_attend(z, mk_t, mv_t, heads=mem_heads, col_block=mem_col).reshape(n * l, d_mem)
        y_s5, hend = s5_core(z.reshape(z_view), s5_ops, h0, width=width, per_row_state=per_row_state)
        h = layer_tail(h, y_s5.reshape(n * l, width), y_mem, w_out_b[0], norm_mlp[0], w_up_b[0], w_down_b[0],
                       glu=(w_glu_b[0], s5_b_glu[0]))
        return h, hend

    def layer1_front(h, n, l, mk_t, mv_t, nb, seq_len):
        z = rms_matmul(h, norm_mix[1], w_in_b[1]).reshape(n, l, d_model)
        y_mem = memory_attend(z, mk_t, mv_t, heads=mem_heads, col_block=mem_col).reshape(n * l, d_mem)
        kv = shared_kv_t(h, norm_kv, wkv_t, wk, wf_t, bf_col, wf, bf_row, place, nb=nb, seq_len=seq_len)
        return z, y_mem, kv

    def layer1_back(h, y_fox, y_mem):
        return layer_tail(h, y_fox, y_mem, w_out_b[1], norm_mlp[1], w_up_b[1], w_down_b[1],
                          final_g=norm_final)

    memkv_t = memory_kv_t(mem_prompt, w_memkv_t)
    m_tok = mem_prompt.shape[1]
    pmk_t = memkv_t[:, :, :d_mem]
    pmv_t = memkv_t[:, :, d_mem:]
    to_mem = lambda a: a.reshape(depth, n_p, mem_heads, mem_dh, m_tok).transpose(0, 1, 4, 2, 3)
    p_mem_k = to_mem(pmk_t)
    p_mem_v = to_mem(pmv_t)

    sw = 2 * per * s5_state
    zeros = jnp.zeros((width // LANES, n_p, 1, sw), F32)
    h, p_hend = layer0(x_prompt, (n_p, seq, d_model), zeros, pmk_t[0], pmv_t[0], False)
    p_hr, p_hi = s5_state_out(p_hend[:, :, 0], per, s5_state)
    z, y_mem, (kt, vt, vtb, ktok, lft, _, aq, ak) = layer1_front(h, n_p, seq, pmk_t[1], pmv_t[1], n_p, seq)
    tokmaj = lambda a: a.reshape(n_p, seq, width)
    y_fox = fox_prompt(z, tokmaj(aq), tokmaj(ktok), tokmaj(ak), vtb, dh=fox_dh, width=width)
    y_prompt = layer1_back(h, y_fox.reshape(n_p * seq, width), y_mem).reshape(n_p, seq, d_model)
    to_heads = lambda a, n, l: a.reshape(n, fox_heads, fox_dh, l).transpose(0, 3, 1, 2)
    p_k = to_heads(kt, n_p, seq)
    p_v = to_heads(vt, n_p, seq)
    p_logf = lft[:, :fox_heads].transpose(0, 2, 1)

    cmk_t = mem_t(cache_mem_k)
    cmv_t = mem_t(cache_mem_v)
    tok = n_s * dec_seq
    s_h0 = s5_state_in(state_s5_re[0], state_s5_im[0], per)[:, None]
    hs, s_hend = layer0(x_sample, (1, tok, d_model), s_h0, cmk_t[0], cmv_t[0], True)
    s_hr, s_hi = s5_state_out(s_hend[:, 0], per, s5_state)
    zs, ys_mem, (kts, vts, _, _, lfts, cums, _, _) = layer1_front(hs, n_s, dec_seq, cmk_t[1], cmv_t[1], 1, dec_seq)
    new_heads = lambda a: a[0].reshape(fox_heads, fox_dh, n_s, dec_seq).transpose(2, 3, 0, 1)
    s_k = new_heads(kts)
    s_v = new_heads(vts)
    s_logf = lfts[0, :fox_heads].reshape(fox_heads, n_s, dec_seq).transpose(1, 2, 0)

    q = zs[..., :width].reshape(n_s, dec_seq, fox_heads, fox_dh) * (fox_dh ** -0.5)
    eye = jnp.eye(fox_heads, dtype=F32)
    qbd = (q.transpose(0, 2, 1, 3)[:, :, :, None, :] * eye[None, :, None, :, None]).reshape(
        n_s, fox_heads * dec_seq, width).astype(BF)
    new_t = jnp.stack([kts[0], vts[0]]).reshape(2, width, n_s, dec_seq).transpose(0, 2, 1, 3)
    new_t = jnp.pad(new_t, ((0, 0), (0, 0), (0, 0), (0, LANES - dec_seq))).astype(BF)
    c_new = cums[:, :fox_heads].reshape(n_s, dec_seq, fox_heads).transpose(0, 2, 1)
    qi = jnp.arange(dec_seq)[:, None]
    tp = jnp.arange(LANES)[None, :]
    c_pad = jnp.pad(c_new, ((0, 0), (0, 0), (0, LANES - dec_seq)))
    bias_new = jnp.where((tp <= qi)[None, None], -c_pad[:, :, None, :], NEG_INF).reshape(
        n_s, fox_heads * dec_seq, LANES)
    decay = past_decay(page_table, logf_pages)
    o_full = fox_decode(page_table, qbd, decay, new_t[0], new_t[1], bias_new, k_pages_t, v_pages_t,
                        heads=fox_heads, lq=dec_seq)
    o5 = o_full.reshape(n_s, fox_heads, dec_seq, fox_heads, fox_dh)
    ys_fox = jnp.einsum("bhqhd->bqhd", o5).reshape(tok, width)
    y_sample = layer1_back(hs, ys_fox, ys_mem).reshape(n_s, dec_seq, d_model)

    return (y_prompt, y_sample, p_hr[None], p_hi[None], p_mem_k, p_mem_v, p_k, p_v, p_logf,
            s_hr[None], s_hi[None], s_k, s_v, s_logf)
```

```python
import functools
import math

import jax
import jax.numpy as jnp
import numpy as np
from jax import lax
from jax.experimental import pallas as pl
from jax.experimental.pallas import tpu as pltpu

BF = jnp.bfloat16
F32 = jnp.float32
RMS_EPS = 1e-6
NEG_INF = -1e30
LOG2E = math.log2(math.e)
V7X_VMEM_BYTES = 64 * 1024 * 1024
VMEM_LIMIT = V7X_VMEM_BYTES - 8 * 1024 * 1024
LANES = 128
SUBLANES = 8
S5_CHUNK = SUBLANES
S5_ROWS_PER_TILE = 256
HEAD_PAD = 16
DEC_PAGES_PER_STEP = 16
FOX_ROWS = 256

NT_DIMS = (((1,), (1,)), ((), ()))


def _params(*sem):
    return pltpu.CompilerParams(dimension_semantics=sem, vmem_limit_bytes=VMEM_LIMIT)


def _dot(a, b):
    return jnp.dot(a, b, preferred_element_type=F32)


def _dot_nt(a, b):
    return lax.dot_general(a, b, NT_DIMS, preferred_element_type=F32)


def _rms(x, g):
    ms = jnp.mean(x * x, axis=-1, keepdims=True)
    return x * lax.rsqrt(ms + RMS_EPS) * g


def _rms_matmul_body(x_ref, g_ref, w_ref, o_ref):
    xn = _rms(x_ref[...], g_ref[...])
    o_ref[...] = _dot(xn.astype(BF), w_ref[...])


def rms_matmul(x, g, w, *, tm=1024):
    t, d = x.shape
    n = w.shape[1]
    tm = min(tm, t)
    return pl.pallas_call(
        _rms_matmul_body,
        grid=(t // tm,),
        in_specs=[pl.BlockSpec((tm, d), lambda i: (i, 0)),
                  pl.BlockSpec((1, d), lambda i: (0, 0)),
                  pl.BlockSpec((d, n), lambda i: (0, 0))],
        out_specs=pl.BlockSpec((tm, n), lambda i: (i, 0)),
        out_shape=jax.ShapeDtypeStruct((t, n), F32),
        compiler_params=_params("parallel"),
        name="rms_matmul",
    )(x, g.reshape(1, d), w)


def _memkv_body(w_ref, m_ref, o_ref):
    o_ref[0, 0] = _dot_nt(w_ref[0], m_ref[0].astype(BF))


def memory_kv_t(mem, w_t):
    nb, m, d = mem.shape
    depth, f, _ = w_t.shape
    return pl.pallas_call(
        _memkv_body,
        grid=(depth, nb),
        in_specs=[pl.BlockSpec((1, f, d), lambda l, b: (l, 0, 0)),
                  pl.BlockSpec((1, m, d), lambda l, b: (b, 0, 0))],
        out_specs=pl.BlockSpec((1, 1, f, m), lambda l, b: (l, b, 0, 0)),
        out_shape=jax.ShapeDtypeStruct((depth, nb, f, m), F32),
        compiler_params=_params("parallel", "parallel"),
        name="memory_kv",
    )(w_t, mem)


def _memattn_body(q_ref, kt_ref, vt_ref, o_ref, *, heads, dh):
    q = q_ref[0] * (dh ** -0.5)
    kt = kt_ref[0].astype(BF)
    vt = vt_ref[0].astype(BF)
    lane = lax.broadcasted_iota(jnp.int32, q.shape, 1)
    out = jnp.zeros(q.shape, F32)
    for h in range(heads):
        sel = (lane >= h * dh) & (lane < (h + 1) * dh)
        s = _dot(jnp.where(sel, q, 0.0).astype(BF), kt)
        m = jnp.max(s, axis=-1, keepdims=True)
        p = jnp.exp(s - m)
        l = jnp.sum(p, axis=-1, keepdims=True)
        o = _dot_nt((p / l).astype(BF), vt)
        out = jnp.where(sel, o, out)
    o_ref[0] = out


def memory_attend(z, kt, vt, *, heads, col_block, tm=512):
    n, l, _ = z.shape
    f, m = kt.shape[1:]
    tm = min(tm, l)
    body = functools.partial(_memattn_body, heads=heads, dh=f // heads)
    return pl.pallas_call(
        body,
        grid=(n, l // tm),
        in_specs=[pl.BlockSpec((1, tm, f), lambda b, i: (b, i, col_block)),
                  pl.BlockSpec((1, f, m), lambda b, i: (b, 0, 0)),
                  pl.BlockSpec((1, f, m), lambda b, i: (b, 0, 0))],
        out_specs=pl.BlockSpec((1, tm, f), lambda b, i: (b, i, 0)),
        out_shape=jax.ShapeDtypeStruct((n, l, f), F32),
        compiler_params=_params("parallel", "parallel"),
        name="memory_attend",
    )(z, kt, vt)


def _cmul(a, h):
    return a[0:1, :] * h + a[1:2, :] * pltpu.roll(h, h.shape[-1] // 2, axis=1)


def _s5_expand(src_ref, dst_ref, *, row_inner, col_inner):
    per = src_ref.shape[1] // LANES
    wide = per * LANES
    r = lax.broadcasted_iota(jnp.int32, (LANES, wide), 0)
    q = lax.broadcasted_iota(jnp.int32, (LANES, wide), 1)
    tile = jnp.where((r // col_inner == q // (per * col_inner)) & (r % col_inner == q % col_inner),
                     1.0, 0.0).astype(BF)
    spread = _dot(src_ref[0], tile).astype(BF)
    qg = (lax.broadcasted_iota(jnp.int32, (row_inner, wide), 1) // col_inner) % per
    for g in range(per):
        for a in range(LANES // row_inner):
            blk = spread[g * LANES + a * row_inner:g * LANES + (a + 1) * row_inner, :]
            dst_ref[pl.ds((a * per + g) * row_inner, row_inner), :] = jnp.where(qg == g, blk, jnp.zeros_like(blk))


def _s5_body(z_ref, kc_ref, wc_ref, oc_ref, apow_ref, h0_ref, y_ref, hend_ref, carry_ref,
             kx_scr, wst_scr, wout_scr, *, rt, per_row_state, group_ch, state):
    i = pl.program_id(2)
    t_len = S5_CHUNK

    @pl.when((pl.program_id(1) == 0) & (i == 0))
    def _():
        _s5_expand(kc_ref, kx_scr, row_inner=group_ch, col_inner=group_ch)
        _s5_expand(wc_ref, wst_scr, row_inner=group_ch, col_inner=state)
        _s5_expand(oc_ref, wout_scr, row_inner=state, col_inner=group_ch)

    lhs = jnp.concatenate([z_ref[0, pl.ds(t, rt, stride=t_len), :].astype(BF) for t in range(t_len)],
                          axis=-1)
    y = _dot(lhs, kx_scr[...])
    hs = _dot(lhs, wst_scr[...])
    a_chunk = apow_ref[0, 0]
    if per_row_state:
        h_in = h0_ref[0, 0]
        hs = hs + _cmul(a_chunk, h_in)
        hend_ref[0, 0] = hs
    else:
        @pl.when(i == 0)
        def _():
            carry_ref[...] = h0_ref[0, 0]
        h0 = carry_ref[...]
        rid = lax.broadcasted_iota(jnp.int32, hs.shape, 0)
        hs = hs + jnp.where(rid == 0, _cmul(a_chunk, h0), 0.0)
        step, k = 1, 0
        while step < rt:
            shifted = jnp.where(rid >= step, pltpu.roll(hs, step, axis=0), 0.0)
            hs = hs + _cmul(apow_ref[0, k], shifted)
            step *= 2
            k += 1
        h_in = jnp.where(rid == 0, h0, pltpu.roll(hs, 1, axis=0))
        carry_ref[...] = hs[rt - 1:rt, :]
        hend_ref[0, 0] = hs[rt - 1:rt, :]
    y = y + _dot(h_in.astype(BF), wout_scr[...])
    for t in range(t_len):
        y_ref[0, pl.ds(t, rt, stride=t_len), :] = y[:, t * LANES:(t + 1) * LANES]


def s5_core(z, ops, h0, *, width, per_row_state):
    kc, wc, oc, apow = ops
    nb, l, _ = z.shape
    nblk = width // LANES
    sw = kc.shape[1]
    state = sw // 2 // (sw // LANES)
    rows_total = l // S5_CHUNK
    rt = rows_total if per_row_state else min(S5_ROWS_PER_TILE, rows_total)
    tiles = rows_total // rt
    rs = h0.shape[2]
    nk = apow.shape[1]
    body = functools.partial(_s5_body, rt=rt, per_row_state=per_row_state,
                             group_ch=LANES // (sw // LANES), state=state)
    compact = pl.BlockSpec((1, sw, LANES), lambda j, b, i: (j, 0, 0))
    return pl.pallas_call(
        body,
        grid=(nblk, nb, tiles),
        in_specs=[pl.BlockSpec((1, rt * S5_CHUNK, LANES), lambda j, b, i: (b, i, j)),
                  compact, compact, compact,
                  pl.BlockSpec((1, nk, 2, sw), lambda j, b, i: (j, 0, 0, 0)),
                  pl.BlockSpec((1, 1, rs, sw), lambda j, b, i: (j, b, 0, 0))],
        out_specs=[pl.BlockSpec((1, rt * S5_CHUNK, LANES), lambda j, b, i: (b, i, j)),
                   pl.BlockSpec((1, 1, rs, sw), lambda j, b, i: (j, b, 0, 0))],
        out_shape=[jax.ShapeDtypeStruct((nb, l, width), F32),
                   jax.ShapeDtypeStruct((nblk, nb, rs, sw), F32)],
        scratch_shapes=[pltpu.VMEM((1, sw), F32), pltpu.VMEM((sw, sw), BF), pltpu.VMEM((sw, sw), BF),
                        pltpu.VMEM((sw, sw), BF)],
        compiler_params=_params("arbitrary", "arbitrary", "arbitrary"),
        name="s5_core",
    )(z, kc, wc, oc, apow, h0)


def s5_operators(a_re, a_im, log_step, b_re, b_im, c_re, c_im, d):
    hp = lax.Precision.HIGHEST
    chunk = S5_CHUNK
    g, p = a_re.shape
    c = b_re.shape[-1]
    per = LANES // c
    nblk = g // per
    dt = jnp.exp(log_step.astype(F32))[:, None]
    a_re = a_re.astype(F32)
    a_im = a_im.astype(F32)
    lam_re = a_re * dt
    lam_im = a_im * dt
    mag = jnp.exp(lam_re)
    ab_re = mag * jnp.cos(lam_im)
    ab_im = mag * jnp.sin(lam_im)
    den = a_re * a_re + a_im * a_im
    nr = ab_re - 1.0
    ni = ab_im
    cr = (nr * a_re + ni * a_im) / den
    ci = (ni * a_re - nr * a_im) / den
    bb_re = cr[..., None] * b_re - ci[..., None] * b_im
    bb_im = cr[..., None] * b_im + ci[..., None] * b_re

    def power(tau):
        tau = tau.astype(F32)[None, :, None]
        m = jnp.exp(lam_re[:, None, :] * tau)
        ph = lam_im[:, None, :] * tau
        return m * jnp.cos(ph), m * jnp.sin(ph)

    pw_re, pw_im = power(jnp.arange(chunk + 1))
    e_re = pw_re[..., None] * bb_re[:, None] - pw_im[..., None] * bb_im[:, None]
    e_im = pw_re[..., None] * bb_im[:, None] + pw_im[..., None] * bb_re[:, None]
    ktau = (jnp.einsum("gop,gtpi->gtoi", c_re, e_re[:, :chunk], precision=hp)
            - jnp.einsum("gop,gtpi->gtoi", c_im, e_im[:, :chunk], precision=hp))
    ktau = ktau.at[:, 0].add(d[:, :, None] * jnp.eye(c, dtype=F32)[None])
    s_idx = jnp.arange(chunk)[:, None]
    t_idx = jnp.arange(chunk)[None, :]
    lag = t_idx - s_idx
    toe = jnp.where((lag >= 0)[None, :, :, None, None], ktau[:, jnp.maximum(lag, 0)], 0.0)
    kc = toe.transpose(0, 1, 4, 2, 3).reshape(nblk, per * chunk * c, chunk * c)

    rev = chunk - 1 - jnp.arange(chunk)
    w_c = jnp.stack([e_re[:, rev], e_im[:, rev]], axis=2)
    wc = w_c.transpose(0, 1, 4, 2, 3).reshape(nblk, per * chunk * c, 2 * p)

    q_re = pw_re[:, 1:chunk + 1]
    q_im = pw_im[:, 1:chunk + 1]
    wo_re = c_re[:, None] * q_re[:, :, None, :] - c_im[:, None] * q_im[:, :, None, :]
    wo_im = -(c_re[:, None] * q_im[:, :, None, :] + c_im[:, None] * q_re[:, :, None, :])
    o_c = jnp.stack([wo_re, wo_im], axis=1)
    oc = o_c.transpose(0, 1, 4, 2, 3).reshape(nblk, per * 2 * p, chunk * c)

    nk = max(1, int(math.log2(S5_ROWS_PER_TILE)))
    sc_re, sc_im = power(chunk * (2 ** jnp.arange(nk)))
    lay = lambda x: x.reshape(nblk, per, nk, p).transpose(0, 2, 1, 3).reshape(nblk, nk, per * p)
    sc_re, sc_im = lay(sc_re), lay(sc_im)
    apow = jnp.stack([jnp.concatenate([sc_re, sc_re], -1), jnp.concatenate([-sc_im, sc_im], -1)], axis=2)
    return kc.astype(BF), wc.astype(BF), oc.astype(BF), apow


def s5_state_in(h_re, h_im, per):
    n, g, p = h_re.shape
    f = lambda x: x.reshape(n, g // per, per * p)
    return jnp.concatenate([f(h_re), f(h_im)], axis=-1).transpose(1, 0, 2)


def s5_state_out(h, per, p):
    nblk, n, w2 = h.shape
    h = h.transpose(1, 0, 2)
    f = lambda x: x.reshape(n, nblk * per, p)
    return f(h[..., :w2 // 2]), f(h[..., w2 // 2:])


def _gelu_tanh(x):
    return 0.5 * x * (1.0 + jnp.tanh(math.sqrt(2.0 / math.pi) * (x + 0.044715 * (x * x * x))))


def _tail_body(*refs, glu, final_norm, width):
    it = iter(refs)
    h_ref, ym_ref, ymem_ref, wo_ref = next(it), next(it), next(it), next(it)
    if glu:
        wg_ref, bg_ref = next(it), next(it)
    gm_ref, wu_ref, wd_ref = next(it), next(it), next(it)
    if final_norm:
        gf_ref = next(it)
    o_ref, xn_scr = next(it), next(it)
    c = pl.program_id(1)

    @pl.when(c == 0)
    def _():
        ym = ym_ref[...]
        if glu:
            zz = _gelu_tanh(ym)
            gate = jax.nn.sigmoid(_dot(zz.astype(BF), wg_ref[...]) + bg_ref[...])
            ym = zz * gate
        d = wo_ref.shape[0]
        h = (h_ref[...] + _dot(ym.astype(BF), wo_ref[0:width, :])
             + _dot(ymem_ref[...].astype(BF), wo_ref[width:d, :]))
        xn_scr[...] = _rms(h, gm_ref[...]).astype(BF)
        o_ref[...] = h

    up = jnp.maximum(_dot(xn_scr[...], wu_ref[...]), 0.0)
    o_ref[...] += _dot((up * up).astype(BF), wd_ref[...])

    if final_norm:
        @pl.when(c == pl.num_programs(1) - 1)
        def _():
            o_ref[...] = _rms(o_ref[...], gf_ref[...])


def layer_tail(h, y_main, y_mem, w_out, g_mlp, w_up, w_down, *, glu=None, final_g=None, tm=1024, ff_chunk=2048):
    t, d = h.shape
    width = y_main.shape[1]
    wm = y_mem.shape[1]
    dff = w_up.shape[1]
    tm = min(tm, t)
    ff_chunk = min(ff_chunk, dff)
    row = lambda i, c: (i, 0)
    fix = lambda i, c: (0, 0)
    args = [h, y_main, y_mem, w_out]
    specs = [pl.BlockSpec((tm, d), row), pl.BlockSpec((tm, width), row), pl.BlockSpec((tm, wm), row),
             pl.BlockSpec((d, d), fix)]
    if glu is not None:
        args += [glu[0], glu[1].reshape(1, width)]
        specs += [pl.BlockSpec((width, width), fix), pl.BlockSpec((1, width), fix)]
    args += [g_mlp.reshape(1, d), w_up, w_down]
    specs += [pl.BlockSpec((1, d), fix), pl.BlockSpec((d, ff_chunk), lambda i, c: (0, c)),
              pl.BlockSpec((ff_chunk, d), lambda i, c: (c, 0))]
    if final_g is not None:
        args.append(final_g.reshape(1, d))
        specs.append(pl.BlockSpec((1, d), fix))
    body = functools.partial(_tail_body, glu=glu is not None, final_norm=final_g is not None, width=width)
    return pl.pallas_call(
        body,
        grid=(t // tm, dff // ff_chunk),
        in_specs=specs,
        out_specs=pl.BlockSpec((tm, d), row),
        out_shape=jax.ShapeDtypeStruct((t, d), F32),
        scratch_shapes=[pltpu.VMEM((tm, d), BF)],
        compiler_params=_params("parallel", "arbitrary"),
        name="layer_tail",
    )(*args)


def _log_sigmoid(x):
    return jnp.minimum(x, 0.0) - jnp.log(1.0 + jnp.exp(-jnp.abs(x)))


def _chop(x):
    return lax.bitcast_convert_type(lax.bitcast_convert_type(x, jnp.uint32) & jnp.uint32(0xFFFF0000), F32)


def _split3(x):
    hi = _chop(x)
    mid = _chop(x - hi)
    lo = x - hi - mid
    return hi.astype(BF), mid.astype(BF), lo.astype(BF)


def _kvf_body(x_ref, g_ref, wkvt_ref, wk_ref, wft_ref, bfc_ref, wf_ref, bfr_ref, pq_ref, cq_ref, pk_ref, ck_ref,
              kt_ref, vt_ref, vtb_ref, ktok_ref, lft_ref, cum_ref, aq_ref, ak_ref, carry_ref,
              *, seq_len, tm, width):
    i = pl.program_id(1)
    hn = _rms(x_ref[...], g_ref[...]).astype(BF)
    kv = _dot_nt(wkvt_ref[...], hn)
    kt_ref[0] = kv[:width]
    vt_ref[0] = kv[width:]
    vtb_ref[0] = kv[width:].astype(BF)
    ktok_ref[...] = _dot(hn, wk_ref[...]).astype(BF)
    lft_ref[0] = _log_sigmoid(_dot_nt(wft_ref[...], hn) + bfc_ref[...])
    lf = _log_sigmoid(_dot(hn, wf_ref[...]) + bfr_ref[...])
    dst = lax.broadcasted_iota(jnp.int32, (tm, tm), 0)
    src = lax.broadcasted_iota(jnp.int32, (tm, tm), 1)
    keep = src <= dst
    if seq_len < tm:
        keep = keep & ((src // seq_len) == (dst // seq_len))
    tri = jnp.where(keep, 1.0, 0.0).astype(BF)
    hi, mid, lo = _split3(lf)
    cum = _dot(tri, hi) + _dot(tri, mid) + _dot(tri, lo)
    if seq_len > tm:
        @pl.when(i % (seq_len // tm) == 0)
        def _():
            carry_ref[...] = jnp.zeros(carry_ref.shape, F32)
        cum = cum + carry_ref[...]
        carry_ref[...] = cum[tm - 1:tm, :]
    cum_ref[...] = cum
    parts = jnp.concatenate(_split3(cum * LOG2E), axis=-1)
    aq_ref[...] = (_dot(parts, pq_ref[...]) + cq_ref[...]).astype(BF)
    ak_ref[...] = (_dot(parts, pk_ref[...]) + ck_ref[...]).astype(BF)


def fox_bias_placement(heads, dh, width):
    per = LANES // dh
    pq = np.zeros((3 * LANES, width), np.float32)
    pk = np.zeros((3 * LANES, width), np.float32)
    cq = np.zeros((1, width), np.float32)
    ck = np.zeros((1, width), np.float32)
    for h in range(heads):
        base = (h // per) * LANES + (h % per) * SUBLANES
        for term in range(3):
            pq[term * LANES + h, base + term] = 1.0
            pk[term * LANES + h, base + 3 + term] = -1.0
        cq[0, base + 3:base + 6] = 1.0
        ck[0, base:base + 3] = 1.0
    return jnp.asarray(pq, BF), jnp.asarray(cq), jnp.asarray(pk, BF), jnp.asarray(ck)


def shared_kv_t(h, g, wkv_t, wk, wf_t, bf_col, wf, bf_row, place, *, nb, seq_len, tm=512):
    t, d = h.shape
    lb = t // nb
    tm = min(tm, lb)
    f2 = wkv_t.shape[0]
    width = f2 // 2
    tiles = lb // tm
    body = functools.partial(_kvf_body, seq_len=seq_len, tm=tm, width=width)
    fix = lambda b, i: (0, 0)
    tok = lambda b, i: (b * tiles + i, 0)
    feat = lambda b, i: (b, 0, i)
    return pl.pallas_call(
        body,
        grid=(nb, tiles),
        in_specs=[pl.BlockSpec((tm, d), tok),
                  pl.BlockSpec((1, d), fix),
                  pl.BlockSpec((f2, d), fix),
                  pl.BlockSpec((d, width), fix),
                  pl.BlockSpec((HEAD_PAD, d), fix),
                  pl.BlockSpec((HEAD_PAD, 1), fix),
                  pl.BlockSpec((d, LANES), fix),
                  pl.BlockSpec((1, LANES), fix),
                  pl.BlockSpec((3 * LANES, width), fix),
                  pl.BlockSpec((1, width), fix),
                  pl.BlockSpec((3 * LANES, width), fix),
                  pl.BlockSpec((1, width), fix)],
        out_specs=[pl.BlockSpec((1, width, tm), feat),
                   pl.BlockSpec((1, width, tm), feat),
                   pl.BlockSpec((1, width, tm), feat),
                   pl.BlockSpec((tm, width), tok),
                   pl.BlockSpec((1, HEAD_PAD, tm), feat),
                   pl.BlockSpec((tm, LANES), tok),
                   pl.BlockSpec((tm, width), tok),
                   pl.BlockSpec((tm, width), tok)],
        out_shape=[jax.ShapeDtypeStruct((nb, width, lb), F32),
                   jax.ShapeDtypeStruct((nb, width, lb), F32),
                   jax.ShapeDtypeStruct((nb, width, lb), BF),
                   jax.ShapeDtypeStruct((t, width), BF),
                   jax.ShapeDtypeStruct((nb, HEAD_PAD, lb), F32),
                   jax.ShapeDtypeStruct((t, LANES), F32),
                   jax.ShapeDtypeStruct((t, width), BF),
                   jax.ShapeDtypeStruct((t, width), BF)],
        scratch_shapes=[pltpu.VMEM((1, LANES), F32)],
        compiler_params=_params("parallel", "arbitrary"),
        name="shared_kv",
    )(h, g.reshape(1, d), wkv_t, wk, wf_t, bf_col, wf, bf_row, *place)


def _fox_body(q_ref, aq_ref, k_ref, ak_ref, vt_ref, o_ref, kcat_scr, vcat_scr, m_scr, acc_scr, st_scr, p_scr,
              *, tq, dh):
    i = pl.program_id(2)
    per = LANES // dh

    @pl.when(i == 0)
    def _():
        kcat_scr[:, :LANES] = k_ref[0]
        kcat_scr[:, LANES:] = ak_ref[0]
        for e in range(per):
            vcat_scr[e, :dh, :] = vt_ref[0, e * dh:(e + 1) * dh, :]
            vcat_scr[e, dh:, :] = jnp.ones((vcat_scr.shape[1] - dh, vcat_scr.shape[2]), BF)

    qcat_t = jnp.concatenate([(q_ref[0] * (dh ** -0.5 * LOG2E)).T.astype(BF),
                              aq_ref[0].astype(F32).T.astype(BF)], axis=0)
    rows_per = min(FOX_ROWS, tq)
    nslice = tq // rows_per
    frow = lax.broadcasted_iota(jnp.int32, qcat_t.shape, 0)
    key_i = lax.broadcasted_iota(jnp.int32, (rows_per, tq), 0)
    qry_i = lax.broadcasted_iota(jnp.int32, (rows_per, tq), 1)
    qes = []
    for e in range(per):
        sel = ((frow >= e * dh) & (frow < (e + 1) * dh)) | (
            (frow >= LANES + e * SUBLANES) & (frow < LANES + (e + 1) * SUBLANES))
        qes.append(jnp.where(sel, qcat_t, jnp.zeros_like(qcat_t)))
    m_scr[...] = jnp.full(m_scr.shape, NEG_INF, F32)
    acc_scr[...] = jnp.zeros(acc_scr.shape, F32)

    def block_rows(j):
        return pl.ds(pl.multiple_of(j * tq, tq), tq)

    def scores(j, slot, e):
        st_scr[slot * per + e] = _dot(kcat_scr[block_rows(j), :], qes[e])

    def attend(j, slot, e, masked):
        def rows(c):
            st = st_scr[slot * per + e, c * rows_per:(c + 1) * rows_per, :]
            if masked:
                st = jnp.where(key_i + c * rows_per <= qry_i, st, NEG_INF)
            return st
        m_prev = m_scr[e:e + 1, :]
        m_new = m_prev
        for c in range(nslice):
            m_new = jnp.maximum(m_new, jnp.max(rows(c), axis=0, keepdims=True))
        alpha = jnp.exp2(m_prev - m_new)
        for c in range(nslice):
            p_scr[e, c * rows_per:(c + 1) * rows_per, :] = jnp.exp2(rows(c) - m_new).astype(BF)
        acc_scr[e] = alpha * acc_scr[e] + _dot(vcat_scr[e, :, block_rows(j)], p_scr[e])
        m_scr[e:e + 1, :] = m_new

    for e in range(per):
        scores(0, 0, e)

    def pair(t, carry):
        for e in range(per):
            scores(2 * t + 1, 1, e)
            attend(2 * t, 0, e, False)
        for e in range(per):
            scores(2 * t + 2, 0, e)
            attend(2 * t + 1, 1, e, False)
        return carry

    lax.fori_loop(0, i // 2, pair, 0)

    @pl.when(i % 2 == 0)
    def _():
        for e in range(per):
            attend(i, 0, e, True)

    @pl.when(i % 2 == 1)
    def _():
        for e in range(per):
            scores(i, 1, e)
            attend(i - 1, 0, e, False)
        for e in range(per):
            attend(i, 1, e, True)

    out_t = jnp.concatenate([acc_scr[e, :dh, :] / acc_scr[e, dh:dh + 1, :] for e in range(per)], axis=0)
    o_ref[0] = out_t.T


def fox_prompt(z, aug_q, k_tok, aug_k, vt_b, *, dh, width, tq=1024):
    n, l, _ = z.shape
    nblk = width // LANES
    tq = min(tq, l)
    body = functools.partial(_fox_body, tq=tq, dh=dh)
    qblk = pl.BlockSpec((1, tq, LANES), lambda b, j, i: (b, i, j))
    seq = pl.BlockSpec((1, l, LANES), lambda b, j, i: (b, 0, j))
    return pl.pallas_call(
        body,
        grid=(n, nblk, l // tq),
        in_specs=[qblk, qblk, seq, seq,
                  pl.BlockSpec((1, LANES, l), lambda b, j, i: (b, j, 0))],
        out_specs=qblk,
        out_shape=jax.ShapeDtypeStruct((n, l, width), F32),
        scratch_shapes=[pltpu.VMEM((l, 2 * LANES), BF), pltpu.VMEM((LANES // dh, dh + HEAD_PAD, l), BF),
                        pltpu.VMEM((LANES // dh, tq), F32), pltpu.VMEM((LANES // dh, dh + HEAD_PAD, tq), F32),
                        pltpu.VMEM((2 * (LANES // dh), tq, tq), F32), pltpu.VMEM((LANES // dh, tq, tq), BF)],
        compiler_params=_params("parallel", "parallel", "arbitrary"),
        name="fox_prompt",
    )(z, aug_q, k_tok, aug_k, vt_b)


def _decode_body(pt_ref, qbd_ref, kn_ref, vn_ref, bn_ref, *refs, g, heads, lq):
    k_pages = refs[:g]
    v_pages = refs[g:2 * g]
    f_pages = refs[2 * g:3 * g]
    o_ref = refs[3 * g]
    m_scr, l_scr, acc_scr, carry_scr = refs[3 * g + 1:]
    j = pl.program_id(1)
    qbd = qbd_ref[0]

    @pl.when(j == 0)
    def _():
        s = _dot(qbd, kn_ref[0]) + bn_ref[0]
        m = jnp.max(s, axis=-1, keepdims=True)
        p = jnp.exp(s - m)
        m_scr[...] = m
        l_scr[...] = jnp.sum(p, axis=-1, keepdims=True)
        acc_scr[...] = _dot_nt(p.astype(BF), vn_ref[0])
        carry_scr[...] = jnp.zeros(carry_scr.shape, F32)

    src = lax.broadcasted_iota(jnp.int32, (LANES, 2 * LANES), 0)
    dst = lax.broadcasted_iota(jnp.int32, (LANES, 2 * LANES), 1)
    w = jnp.where((dst >= LANES) | (src > dst), 1.0, 0.0).astype(BF)
    terms = jnp.concatenate([t for i in range(g) for t in _split3(f_pages[i][0])], axis=0)
    r_all = _dot(terms, w)
    carry = carry_scr[...]
    decay = [None] * g
    for i in reversed(range(g)):
        base = i * 3 * HEAD_PAD
        r = (r_all[base:base + HEAD_PAD] + r_all[base + HEAD_PAD:base + 2 * HEAD_PAD]
             + r_all[base + 2 * HEAD_PAD:base + 3 * HEAD_PAD])
        decay[i] = r[:, :LANES] + carry
        carry = carry + r[:, LANES:]
    carry_scr[...] = carry
    decay = jnp.concatenate(decay, axis=-1)

    s = jnp.concatenate([_dot(qbd, k_pages[i][0].astype(BF)) for i in range(g)], axis=-1)
    rows, cols = s.shape
    s = (s.reshape(heads, lq, cols) + decay[:heads, None, :]).reshape(rows, cols)
    m_prev = m_scr[...]
    m_new = jnp.maximum(m_prev, jnp.max(s, axis=-1, keepdims=True))
    alpha = jnp.exp(m_prev - m_new)
    p = jnp.exp(s - m_new).astype(BF)
    l_scr[...] = alpha * l_scr[...] + jnp.sum(p.astype(F32), axis=-1, keepdims=True)
    acc = alpha * acc_scr[...]
    page = cols // g
    for i in range(g):
        acc = acc + _dot_nt(p[:, i * page:(i + 1) * page], v_pages[i][0].astype(BF))
    acc_scr[...] = acc
    m_scr[...] = m_new

    @pl.when(j == pl.num_programs(1) - 1)
    def _():
        o_ref[0] = acc_scr[...] / l_scr[...]


def fox_decode(page_table, qbd, kn_t, vn_t, bias_new, k_pages_t, v_pages_t, logf_pages, *, heads, lq):
    nb, npg = page_table.shape
    rows, width = qbd.shape[1:]
    page = k_pages_t.shape[-1]
    g = min(DEC_PAGES_PER_STEP, npg)
    steps = npg // g

    def page_spec(i, sublanes):
        return pl.BlockSpec((1, sublanes, page), lambda b, j, pt: (pt[b, (steps - 1 - j) * g + i], 0, 0))

    fixed = lambda b, j, pt: (b, 0, 0)
    grid_spec = pltpu.PrefetchScalarGridSpec(
        num_scalar_prefetch=1,
        grid=(nb, steps),
        in_specs=[pl.BlockSpec((1, rows, width), fixed),
                  pl.BlockSpec((1, width, LANES), fixed),
                  pl.BlockSpec((1, width, LANES), fixed),
                  pl.BlockSpec((1, rows, LANES), fixed)]
                 + [page_spec(i, width) for i in range(g)] + [page_spec(i, width) for i in range(g)]
                 + [page_spec(i, HEAD_PAD) for i in range(g)],
        out_specs=pl.BlockSpec((1, rows, width), fixed),
        scratch_shapes=[pltpu.VMEM((rows, 1), F32), pltpu.VMEM((rows, 1), F32),
                        pltpu.VMEM((rows, width), F32), pltpu.VMEM((HEAD_PAD, LANES), F32)],
    )
    return pl.pallas_call(
        functools.partial(_decode_body, g=g, heads=heads, lq=lq),
        grid_spec=grid_spec,
        out_shape=jax.ShapeDtypeStruct((nb, rows, width), F32),
        compiler_params=_params("parallel", "arbitrary"),
        name="fox_decode",
    )(page_table, qbd, kn_t, vn_t, bias_new, *([k_pages_t] * g), *([v_pages_t] * g), *([logf_pages] * g))


def kernel(x_prompt, x_sample, state_s5_re, state_s5_im, cache_mem_k, cache_mem_v, cache_k, cache_v, cache_logf, page_table, mem_prompt, norm_mix, norm_mlp, w_in, w_out, w_up, w_down, w_mem_kv, s5_a_re, s5_a_im, s5_log_step, s5_b_re, s5_b_im, s5_c_re, s5_c_im, s5_d, s5_w_glu, s5_b_glu, norm_kv, w_kv, w_f, b_f, norm_final):
    n_p, seq, d_model = x_prompt.shape
    n_s, dec_seq, _ = x_sample.shape
    depth = w_in.shape[0]
    n_a = s5_a_re.shape[0]
    assert depth == 2 and n_a == 1, "layer pattern: one S5 layer then one FoX layer"
    mem_heads, mem_dh = cache_mem_k.shape[3:]
    d_mem = mem_heads * mem_dh
    width = d_model - d_mem
    fox_heads, fox_dh = cache_k.shape[2:]
    page = cache_k.shape[1]
    group_ch = s5_b_re.shape[-1]
    s5_state = s5_a_re.shape[-1]
    per = LANES // group_ch
    mem_col = width // d_mem
    assert width % d_mem == 0 and width % LANES == 0 and LANES % fox_dh == 0 and LANES % group_ch == 0
    assert fox_heads <= HEAD_PAD and page == LANES
    assert dec_seq == S5_CHUNK and seq % S5_CHUNK == 0
    assert S5_CHUNK * group_ch == LANES and 2 * s5_state == LANES
    assert (seq // S5_CHUNK) % min(S5_ROWS_PER_TILE, seq // S5_CHUNK) == 0

    w_in_b = w_in.astype(BF)
    w_out_b = w_out.astype(BF)
    w_up_b = w_up.astype(BF)
    w_down_b = w_down.astype(BF)
    w_glu_b = s5_w_glu.astype(BF)
    w_memkv_t = w_mem_kv.transpose(0, 2, 1).astype(BF)
    wkv_t = w_kv.T.astype(BF)
    wk = w_kv[:, :width].astype(BF)
    wf_t = jnp.pad(w_f.T, ((0, HEAD_PAD - fox_heads), (0, 0))).astype(BF)
    bf_col = jnp.pad(b_f, (0, HEAD_PAD - fox_heads)).reshape(HEAD_PAD, 1)
    wf = jnp.pad(w_f, ((0, 0), (0, LANES - fox_heads))).astype(BF)
    bf_row = jnp.pad(b_f, (0, LANES - fox_heads)).reshape(1, LANES)
    place = fox_bias_placement(fox_heads, fox_dh, width)

    def mem_t(c):
        return c.transpose(0, 1, 3, 4, 2).reshape(c.shape[0], c.shape[1], d_mem, c.shape[2])

    k_pages_t = cache_k.transpose(0, 2, 3, 1).reshape(cache_k.shape[0], width, page)
    v_pages_t = cache_v.transpose(0, 2, 3, 1).reshape(cache_v.shape[0], width, page)
    logf_pages = jnp.pad(cache_logf.transpose(0, 2, 1), ((0, 0), (0, HEAD_PAD - fox_heads), (0, 0)))

    s5_ops = s5_operators(s5_a_re[0], s5_a_im[0], s5_log_step[0], s5_b_re[0], s5_b_im[0],
                          s5_c_re[0], s5_c_im[0], s5_d[0])

    def layer0(x, z_view, h0, mk_t, mv_t, per_row_state):
        n, l, _ = x.shape
        h = x.reshape(n * l, d_model)
        z = rms_matmul(h, norm_mix[0], w_in_b[0]).reshape(n, l, d_model)
        y_mem = memory_attend(z, mk_t, mv_t, heads=mem_heads, col_block=mem_col).reshape(n * l, d_mem)
        y_s5, hend = s5_core(z.reshape(z_view), s5_ops, h0, width=width, per_row_state=per_row_state)
        h = layer_tail(h, y_s5.reshape(n * l, width), y_mem, w_out_b[0], norm_mlp[0], w_up_b[0], w_down_b[0],
                       glu=(w_glu_b[0], s5_b_glu[0]))
        return h, hend

    def layer1_front(h, n, l, mk_t, mv_t, nb, seq_len):
        z = rms_matmul(h, norm_mix[1], w_in_b[1]).reshape(n, l, d_model)
        y_mem = memory_attend(z, mk_t, mv_t, heads=mem_heads, col_block=mem_col).reshape(n * l, d_mem)
        kv = shared_kv_t(h, norm_kv, wkv_t, wk, wf_t, bf_col, wf, bf_row, place, nb=nb, seq_len=seq_len)
        return z, y_mem, kv

    def layer1_back(h, y_fox, y_mem):
        return layer_tail(h, y_fox, y_mem, w_out_b[1], norm_mlp[1], w_up_b[1], w_down_b[1],
                          final_g=norm_final)

    memkv_t = memory_kv_t(mem_prompt, w_memkv_t)
    m_tok = mem_prompt.shape[1]
    pmk_t = memkv_t[:, :, :d_mem]
    pmv_t = memkv_t[:, :, d_mem:]
    to_mem = lambda a: a.reshape(depth, n_p, mem_heads, mem_dh, m_tok).transpose(0, 1, 4, 2, 3)
    p_mem_k = to_mem(pmk_t)
    p_mem_v = to_mem(pmv_t)

    sw = 2 * per * s5_state
    zeros = jnp.zeros((width // LANES, n_p, 1, sw), F32)
    h, p_hend = layer0(x_prompt, (n_p, seq, d_model), zeros, pmk_t[0], pmv_t[0], False)
    p_hr, p_hi = s5_state_out(p_hend[:, :, 0], per, s5_state)
    z, y_mem, (kt, vt, vtb, ktok, lft, _, aq, ak) = layer1_front(h, n_p, seq, pmk_t[1], pmv_t[1], n_p, seq)
    tokmaj = lambda a: a.reshape(n_p, seq, width)
    y_fox = fox_prompt(z, tokmaj(aq), tokmaj(ktok), tokmaj(ak), vtb, dh=fox_dh, width=width)
    y_prompt = layer1_back(h, y_fox.reshape(n_p * seq, width), y_mem).reshape(n_p, seq, d_model)
    to_heads = lambda a, n, l: a.reshape(n, fox_heads, fox_dh, l).transpose(0, 3, 1, 2)
    p_k = to_heads(kt, n_p, seq)
    p_v = to_heads(vt, n_p, seq)
    p_logf = lft[:, :fox_heads].transpose(0, 2, 1)

    cmk_t = mem_t(cache_mem_k)
    cmv_t = mem_t(cache_mem_v)
    tok = n_s * dec_seq
    s_h0 = s5_state_in(state_s5_re[0], state_s5_im[0], per)[:, None]
    hs, s_hend = layer0(x_sample, (1, tok, d_model), s_h0, cmk_t[0], cmv_t[0], True)
    s_hr, s_hi = s5_state_out(s_hend[:, 0], per, s5_state)
    zs, ys_mem, (kts, vts, _, _, lfts, cums, _, _) = layer1_front(hs, n_s, dec_seq, cmk_t[1], cmv_t[1], 1, dec_seq)
    new_heads = lambda a: a[0].reshape(fox_heads, fox_dh, n_s, dec_seq).transpose(2, 3, 0, 1)
    s_k = new_heads(kts)
    s_v = new_heads(vts)
    s_logf = lfts[0, :fox_heads].reshape(fox_heads, n_s, dec_seq).transpose(1, 2, 0)

    q = zs[..., :width].reshape(n_s, dec_seq, fox_heads, fox_dh) * (fox_dh ** -0.5)
    eye = jnp.eye(fox_heads, dtype=F32)
    qbd = (q.transpose(0, 2, 1, 3)[:, :, :, None, :] * eye[None, :, None, :, None]).reshape(
        n_s, fox_heads * dec_seq, width).astype(BF)
    new_t = jnp.stack([kts[0], vts[0]]).reshape(2, width, n_s, dec_seq).transpose(0, 2, 1, 3)
    new_t = jnp.pad(new_t, ((0, 0), (0, 0), (0, 0), (0, LANES - dec_seq))).astype(BF)
    c_new = cums[:, :fox_heads].reshape(n_s, dec_seq, fox_heads).transpose(0, 2, 1)
    qi = jnp.arange(dec_seq)[:, None]
    tp = jnp.arange(LANES)[None, :]
    c_pad = jnp.pad(c_new, ((0, 0), (0, 0), (0, LANES - dec_seq)))
    bias_new = jnp.where((tp <= qi)[None, None], -c_pad[:, :, None, :], NEG_INF).reshape(
        n_s, fox_heads * dec_seq, LANES)
    o_full = fox_decode(page_table, qbd, new_t[0], new_t[1], bias_new, k_pages_t, v_pages_t, logf_pages,
                        heads=fox_heads, lq=dec_seq)
    o5 = o_full.reshape(n_s, fox_heads, dec_seq, fox_heads, fox_dh)
    ys_fox = jnp.einsum("bhqhd->bqhd", o5).reshape(tok, width)
    y_sample = layer1_back(hs, ys_fox, ys_mem).reshape(n_s, dec_seq, d_model)

    return (y_prompt, y_sample, p_hr[None], p_hi[None], p_mem_k, p_mem_v, p_k, p_v, p_logf,
            s_hr[None], s_hi[None], s_k, s_v, s_logf)
```

```python
import functools
import math

import jax
import jax.numpy as jnp
import numpy as np
from jax import lax
from jax.experimental import pallas as pl
from jax.experimental.pallas import tpu as pltpu

BF = jnp.bfloat16
F32 = jnp.float32
RMS_EPS = 1e-6
NEG_INF = -1e30
LOG2E = math.log2(math.e)
V7X_VMEM_BYTES = 64 * 1024 * 1024
VMEM_LIMIT = V7X_VMEM_BYTES - 8 * 1024 * 1024
LANES = 128
SUBLANES = 8
S5_CHUNK = SUBLANES
S5_ROWS_PER_TILE = 256
HEAD_PAD = 16
DEC_PAGES_PER_STEP = 16
MEM_SEQS_PER_STEP = 8
FOX_ROWS = 256

NT_DIMS = (((1,), (1,)), ((), ()))


def _params(*sem):
    return pltpu.CompilerParams(dimension_semantics=sem, vmem_limit_bytes=VMEM_LIMIT)


def _dot(a, b):
    return jnp.dot(a, b, preferred_element_type=F32)


def _dot_nt(a, b):
    return lax.dot_general(a, b, NT_DIMS, preferred_element_type=F32)


def _rms(x, g):
    ms = jnp.mean(x * x, axis=-1, keepdims=True)
    return x * lax.rsqrt(ms + RMS_EPS) * g


def _rms_matmul_body(x_ref, g_ref, w_ref, o_ref):
    xn = _rms(x_ref[...], g_ref[0])
    o_ref[...] = _dot(xn.astype(BF), w_ref[0])


def rms_matmul(x, g, w, layer, *, tm=1024):
    t, d = x.shape
    n = w.shape[2]
    tm = min(tm, t)
    return pl.pallas_call(
        _rms_matmul_body,
        grid=(t // tm,),
        in_specs=[pl.BlockSpec((tm, d), lambda i: (i, 0)),
                  pl.BlockSpec((1, 1, d), lambda i: (layer, 0, 0)),
                  pl.BlockSpec((1, d, n), lambda i: (layer, 0, 0))],
        out_specs=pl.BlockSpec((tm, n), lambda i: (i, 0)),
        out_shape=jax.ShapeDtypeStruct((t, n), F32),
        compiler_params=_params("parallel"),
        name="rms_matmul",
    )(x, g, w)


def _memkv_body(w_ref, m_ref, o_ref):
    o_ref[0, 0] = _dot_nt(w_ref[0], m_ref[0].astype(BF))


def memory_kv_t(mem, w_t):
    nb, m, d = mem.shape
    depth, f, _ = w_t.shape
    return pl.pallas_call(
        _memkv_body,
        grid=(depth, nb),
        in_specs=[pl.BlockSpec((1, f, d), lambda l, b: (l, 0, 0)),
                  pl.BlockSpec((1, m, d), lambda l, b: (b, 0, 0))],
        out_specs=pl.BlockSpec((1, 1, f, m), lambda l, b: (l, b, 0, 0)),
        out_shape=jax.ShapeDtypeStruct((depth, nb, f, m), F32),
        compiler_params=_params("parallel", "parallel"),
        name="memory_kv",
    )(w_t, mem)


def _memattn_body(q_ref, kt_ref, vt_ref, o_ref, *, heads, dh):
    for sq in range(q_ref.shape[0]):
        q = q_ref[sq] * (dh ** -0.5)
        kt = kt_ref[0, sq].astype(BF)
        vt = vt_ref[0, sq].astype(BF)
        lane = lax.broadcasted_iota(jnp.int32, q.shape, 1)
        out = jnp.zeros(q.shape, F32)
        for h in range(heads):
            sel = (lane >= h * dh) & (lane < (h + 1) * dh)
            s = _dot(jnp.where(sel, q, 0.0).astype(BF), kt)
            m = jnp.max(s, axis=-1, keepdims=True)
            p = jnp.exp(s - m)
            l = jnp.sum(p, axis=-1, keepdims=True)
            o = _dot_nt((p / l).astype(BF), vt)
            out = jnp.where(sel, o, out)
        o_ref[sq] = out


def memory_attend(z, kt, vt, layer, *, heads, f, col_block, k_row_block=0, v_row_block=0, tm=512):
    n, l, _ = z.shape
    m = kt.shape[-1]
    tm = min(tm, l)
    seqs = MEM_SEQS_PER_STEP if (l == tm and n % MEM_SEQS_PER_STEP == 0) else 1
    body = functools.partial(_memattn_body, heads=heads, dh=f // heads)
    return pl.pallas_call(
        body,
        grid=(n // seqs, l // tm),
        in_specs=[pl.BlockSpec((seqs, tm, f), lambda b, i: (b, i, col_block)),
                  pl.BlockSpec((1, seqs, f, m), lambda b, i: (layer, b, k_row_block, 0)),
                  pl.BlockSpec((1, seqs, f, m), lambda b, i: (layer, b, v_row_block, 0))],
        out_specs=pl.BlockSpec((seqs, tm, f), lambda b, i: (b, i, 0)),
        out_shape=jax.ShapeDtypeStruct((n, l, f), F32),
        compiler_params=_params("parallel", "parallel"),
        name="memory_attend",
    )(z, kt, vt)


def _cmul(a, h):
    return a[0:1, :] * h + a[1:2, :] * pltpu.roll(h, h.shape[-1] // 2, axis=1)


def _s5_expand(src_ref, dst_ref, *, row_inner, col_inner):
    per = src_ref.shape[1] // LANES
    wide = per * LANES
    r = lax.broadcasted_iota(jnp.int32, (LANES, wide), 0)
    q = lax.broadcasted_iota(jnp.int32, (LANES, wide), 1)
    tile = jnp.where((r // col_inner == q // (per * col_inner)) & (r % col_inner == q % col_inner),
                     1.0, 0.0).astype(BF)
    spread = _dot(src_ref[0], tile).astype(BF)
    qg = (lax.broadcasted_iota(jnp.int32, (row_inner, wide), 1) // col_inner) % per
    for g in range(per):
        for a in range(LANES // row_inner):
            blk = spread[g * LANES + a * row_inner:g * LANES + (a + 1) * row_inner, :]
            dst_ref[pl.ds((a * per + g) * row_inner, row_inner), :] = jnp.where(qg == g, blk, jnp.zeros_like(blk))


def _s5_body(z_ref, kc_ref, wc_ref, oc_ref, apow_ref, h0_ref, y_ref, hend_ref, carry_ref,
             kx_scr, wst_scr, wout_scr, *, rt, per_row_state, group_ch, state):
    i = pl.program_id(2)
    t_len = S5_CHUNK

    @pl.when((pl.program_id(1) == 0) & (i == 0))
    def _():
        _s5_expand(kc_ref, kx_scr, row_inner=group_ch, col_inner=group_ch)
        _s5_expand(wc_ref, wst_scr, row_inner=group_ch, col_inner=state)
        _s5_expand(oc_ref, wout_scr, row_inner=state, col_inner=group_ch)

    lhs = jnp.concatenate([z_ref[0, pl.ds(t, rt, stride=t_len), :].astype(BF) for t in range(t_len)],
                          axis=-1)
    y = _dot(lhs, kx_scr[...])
    hs = _dot(lhs, wst_scr[...])
    a_chunk = apow_ref[0, 0]
    if per_row_state:
        h_in = h0_ref[0, 0]
        hs = hs + _cmul(a_chunk, h_in)
        hend_ref[0, 0] = hs
    else:
        @pl.when(i == 0)
        def _():
            carry_ref[...] = h0_ref[0, 0]
        h0 = carry_ref[...]
        rid = lax.broadcasted_iota(jnp.int32, hs.shape, 0)
        hs = hs + jnp.where(rid == 0, _cmul(a_chunk, h0), 0.0)
        step, k = 1, 0
        while step < rt:
            shifted = jnp.where(rid >= step, pltpu.roll(hs, step, axis=0), 0.0)
            hs = hs + _cmul(apow_ref[0, k], shifted)
            step *= 2
            k += 1
        h_in = jnp.where(rid == 0, h0, pltpu.roll(hs, 1, axis=0))
        carry_ref[...] = hs[rt - 1:rt, :]
        hend_ref[0, 0] = hs[rt - 1:rt, :]
    y = y + _dot(h_in.astype(BF), wout_scr[...])
    for t in range(t_len):
        y_ref[0, pl.ds(t, rt, stride=t_len), :] = y[:, t * LANES:(t + 1) * LANES]


def s5_core(z, ops, h0, *, width, per_row_state):
    kc, wc, oc, apow = ops
    nb, l, _ = z.shape
    nblk = width // LANES
    sw = kc.shape[1]
    state = sw // 2 // (sw // LANES)
    rows_total = l // S5_CHUNK
    rt = rows_total if per_row_state else min(S5_ROWS_PER_TILE, rows_total)
    tiles = rows_total // rt
    rs = h0.shape[2]
    nk = apow.shape[1]
    body = functools.partial(_s5_body, rt=rt, per_row_state=per_row_state,
                             group_ch=LANES // (sw // LANES), state=state)
    compact = pl.BlockSpec((1, sw, LANES), lambda j, b, i: (j, 0, 0))
    return pl.pallas_call(
        body,
        grid=(nblk, nb, tiles),
        in_specs=[pl.BlockSpec((1, rt * S5_CHUNK, LANES), lambda j, b, i: (b, i, j)),
                  compact, compact, compact,
                  pl.BlockSpec((1, nk, 2, sw), lambda j, b, i: (j, 0, 0, 0)),
                  pl.BlockSpec((1, 1, rs, sw), lambda j, b, i: (j, b, 0, 0))],
        out_specs=[pl.BlockSpec((1, rt * S5_CHUNK, LANES), lambda j, b, i: (b, i, j)),
                   pl.BlockSpec((1, 1, rs, sw), lambda j, b, i: (j, b, 0, 0))],
        out_shape=[jax.ShapeDtypeStruct((nb, l, width), F32),
                   jax.ShapeDtypeStruct((nblk, nb, rs, sw), F32)],
        scratch_shapes=[pltpu.VMEM((1, sw), F32), pltpu.VMEM((sw, sw), BF), pltpu.VMEM((sw, sw), BF),
                        pltpu.VMEM((sw, sw), BF)],
        compiler_params=_params("arbitrary", "arbitrary", "arbitrary"),
        name="s5_core",
    )(z, kc, wc, oc, apow, h0)


def s5_operators(a_re, a_im, log_step, b_re, b_im, c_re, c_im, d):
    chunk = S5_CHUNK
    g, p = a_re.shape
    c = b_re.shape[-1]
    per = LANES // c
    nblk = g // per
    dt = jnp.exp(log_step.astype(F32))[:, None]
    a_re = a_re.astype(F32)
    a_im = a_im.astype(F32)
    lam_re = a_re * dt
    lam_im = a_im * dt
    mag = jnp.exp(lam_re)
    ab_re = mag * jnp.cos(lam_im)
    ab_im = mag * jnp.sin(lam_im)
    den = a_re * a_re + a_im * a_im
    nr = ab_re - 1.0
    ni = ab_im
    cr = (nr * a_re + ni * a_im) / den
    ci = (ni * a_re - nr * a_im) / den
    bb_re = cr[..., None] * b_re - ci[..., None] * b_im
    bb_im = cr[..., None] * b_im + ci[..., None] * b_re

    def power(tau):
        tau = tau.astype(F32)[None, :, None]
        m = jnp.exp(lam_re[:, None, :] * tau)
        ph = lam_im[:, None, :] * tau
        return m * jnp.cos(ph), m * jnp.sin(ph)

    pw_re, pw_im = power(jnp.arange(chunk + 1))
    e_re = pw_re[..., None] * bb_re[:, None] - pw_im[..., None] * bb_im[:, None]
    e_im = pw_re[..., None] * bb_im[:, None] + pw_im[..., None] * bb_re[:, None]
    et_re = e_re[:, :chunk].transpose(0, 1, 3, 2)[:, :, None]
    et_im = e_im[:, :chunk].transpose(0, 1, 3, 2)[:, :, None]
    ktau = jnp.sum(c_re[:, None, :, None, :] * et_re - c_im[:, None, :, None, :] * et_im, axis=-1)
    ktau = ktau.at[:, 0].add(d[:, :, None] * jnp.eye(c, dtype=F32)[None])
    s_idx = jnp.arange(chunk)[:, None]
    t_idx = jnp.arange(chunk)[None, :]
    lag = t_idx - s_idx
    toe = jnp.where((lag >= 0)[None, :, :, None, None], ktau[:, jnp.maximum(lag, 0)], 0.0)
    kc = toe.transpose(0, 1, 4, 2, 3).reshape(nblk, per * chunk * c, chunk * c)

    rev = chunk - 1 - jnp.arange(chunk)
    w_c = jnp.stack([e_re[:, rev], e_im[:, rev]], axis=2)
    wc = w_c.transpose(0, 1, 4, 2, 3).reshape(nblk, per * chunk * c, 2 * p)

    q_re = pw_re[:, 1:chunk + 1]
    q_im = pw_im[:, 1:chunk + 1]
    wo_re = c_re[:, None] * q_re[:, :, None, :] - c_im[:, None] * q_im[:, :, None, :]
    wo_im = -(c_re[:, None] * q_im[:, :, None, :] + c_im[:, None] * q_re[:, :, None, :])
    o_c = jnp.stack([wo_re, wo_im], axis=1)
    oc = o_c.transpose(0, 1, 4, 2, 3).reshape(nblk, per * 2 * p, chunk * c)

    nk = max(1, int(math.log2(S5_ROWS_PER_TILE)))
    sc_re, sc_im = power(chunk * (2 ** jnp.arange(nk)))
    lay = lambda x: x.reshape(nblk, per, nk, p).transpose(0, 2, 1, 3).reshape(nblk, nk, per * p)
    sc_re, sc_im = lay(sc_re), lay(sc_im)
    apow = jnp.stack([jnp.concatenate([sc_re, sc_re], -1), jnp.concatenate([-sc_im, sc_im], -1)], axis=2)
    return kc.astype(BF), wc.astype(BF), oc.astype(BF), apow


def s5_state_in(h_re, h_im, per):
    n, g, p = h_re.shape
    f = lambda x: x.reshape(n, g // per, per * p)
    return jnp.concatenate([f(h_re), f(h_im)], axis=-1).transpose(1, 0, 2)


def s5_state_out(h, per, p):
    nblk, n, w2 = h.shape
    h = h.transpose(1, 0, 2)
    f = lambda x: x.reshape(n, nblk * per, p)
    return f(h[..., :w2 // 2]), f(h[..., w2 // 2:])


def _gelu_tanh(x):
    return 0.5 * x * (1.0 + jnp.tanh(math.sqrt(2.0 / math.pi) * (x + 0.044715 * (x * x * x))))


def _tail_body(*refs, glu, final_norm, width):
    it = iter(refs)
    h_ref, ym_ref, ymem_ref, wo_ref = next(it), next(it), next(it), next(it)
    if glu:
        wg_ref, bg_ref = next(it), next(it)
    gm_ref, wu_ref, wd_ref = next(it), next(it), next(it)
    if final_norm:
        gf_ref = next(it)
    o_ref, xn_scr = next(it), next(it)
    c = pl.program_id(1)

    @pl.when(c == 0)
    def _():
        ym = ym_ref[...]
        if glu:
            zz = _gelu_tanh(ym)
            gate = jax.nn.sigmoid(_dot(zz.astype(BF), wg_ref[...]) + bg_ref[...])
            ym = zz * gate
        d = wo_ref.shape[1]
        h = (h_ref[...] + _dot(ym.astype(BF), wo_ref[0, 0:width, :])
             + _dot(ymem_ref[...].astype(BF), wo_ref[0, width:d, :]))
        xn_scr[...] = _rms(h, gm_ref[0]).astype(BF)
        o_ref[...] = h

    up = jnp.maximum(_dot(xn_scr[...], wu_ref[0]), 0.0)
    o_ref[...] += _dot((up * up).astype(BF), wd_ref[0])

    if final_norm:
        @pl.when(c == pl.num_programs(1) - 1)
        def _():
            o_ref[...] = _rms(o_ref[...], gf_ref[...])


def layer_tail(h, y_main, y_mem, w_out, g_mlp, w_up, w_down, layer, *, glu=None, final_g=None, tm=1024,
               ff_chunk=2048):
    t, d = h.shape
    width = y_main.shape[1]
    wm = y_mem.shape[1]
    dff = w_up.shape[2]
    tm = min(tm, t)
    ff_chunk = min(ff_chunk, dff)
    row = lambda i, c: (i, 0)
    fix = lambda i, c: (0, 0)
    args = [h, y_main, y_mem, w_out]
    specs = [pl.BlockSpec((tm, d), row), pl.BlockSpec((tm, width), row), pl.BlockSpec((tm, wm), row),
             pl.BlockSpec((1, d, d), lambda i, c: (layer, 0, 0))]
    if glu is not None:
        args += [glu[0], glu[1].reshape(1, width)]
        specs += [pl.BlockSpec((width, width), fix), pl.BlockSpec((1, width), fix)]
    args += [g_mlp, w_up, w_down]
    specs += [pl.BlockSpec((1, 1, d), lambda i, c: (layer, 0, 0)),
              pl.BlockSpec((1, d, ff_chunk), lambda i, c: (layer, 0, c)),
              pl.BlockSpec((1, ff_chunk, d), lambda i, c: (layer, c, 0))]
    if final_g is not None:
        args.append(final_g.reshape(1, d))
        specs.append(pl.BlockSpec((1, d), fix))
    body = functools.partial(_tail_body, glu=glu is not None, final_norm=final_g is not None, width=width)
    return pl.pallas_call(
        body,
        grid=(t // tm, dff // ff_chunk),
        in_specs=specs,
        out_specs=pl.BlockSpec((tm, d), row),
        out_shape=jax.ShapeDtypeStruct((t, d), F32),
        scratch_shapes=[pltpu.VMEM((tm, d), BF)],
        compiler_params=_params("parallel", "arbitrary"),
        name="layer_tail",
    )(*args)


def _log_sigmoid(x):
    return jnp.minimum(x, 0.0) - jnp.log(1.0 + jnp.exp(-jnp.abs(x)))


def _chop(x):
    return lax.bitcast_convert_type(lax.bitcast_convert_type(x, jnp.uint32) & jnp.uint32(0xFFFF0000), F32)


def _split3(x):
    hi = _chop(x)
    mid = _chop(x - hi)
    lo = x - hi - mid
    return hi.astype(BF), mid.astype(BF), lo.astype(BF)


def _kvf_body(x_ref, g_ref, wkvt_ref, wk_ref, wft_ref, bfc_ref, wf_ref, bfr_ref, pq_ref, cq_ref, pk_ref, ck_ref,
              kt_ref, vt_ref, vtb_ref, ktok_ref, lft_ref, cum_ref, aq_ref, ak_ref, carry_ref,
              *, seq_len, tm, width):
    i = pl.program_id(1)
    hn = _rms(x_ref[...], g_ref[...]).astype(BF)
    kv = _dot_nt(wkvt_ref[...], hn)
    kt_ref[0] = kv[:width]
    vt_ref[0] = kv[width:]
    vtb_ref[0] = kv[width:].astype(BF)
    ktok_ref[...] = _dot(hn, wk_ref[...]).astype(BF)
    lft_ref[0] = _log_sigmoid(_dot_nt(wft_ref[...], hn) + bfc_ref[...])
    lf = _log_sigmoid(_dot(hn, wf_ref[...]) + bfr_ref[...])
    dst = lax.broadcasted_iota(jnp.int32, (tm, tm), 0)
    src = lax.broadcasted_iota(jnp.int32, (tm, tm), 1)
    keep = src <= dst
    if seq_len < tm:
        keep = keep & ((src // seq_len) == (dst // seq_len))
    tri = jnp.where(keep, 1.0, 0.0).astype(BF)
    hi, mid, lo = _split3(lf)
    cum = _dot(tri, hi) + _dot(tri, mid) + _dot(tri, lo)
    if seq_len > tm:
        @pl.when(i % (seq_len // tm) == 0)
        def _():
            carry_ref[...] = jnp.zeros(carry_ref.shape, F32)
        cum = cum + carry_ref[...]
        carry_ref[...] = cum[tm - 1:tm, :]
    cum_ref[...] = cum
    parts = jnp.concatenate(_split3(cum * LOG2E), axis=-1)
    aq_ref[...] = (_dot(parts, pq_ref[...]) + cq_ref[...]).astype(BF)
    ak_ref[...] = (_dot(parts, pk_ref[...]) + ck_ref[...]).astype(BF)


def fox_bias_placement(heads, dh, width):
    per = LANES // dh
    pq = np.zeros((3 * LANES, width), np.float32)
    pk = np.zeros((3 * LANES, width), np.float32)
    cq = np.zeros((1, width), np.float32)
    ck = np.zeros((1, width), np.float32)
    for h in range(heads):
        base = (h // per) * LANES + (h % per) * SUBLANES
        for term in range(3):
            pq[term * LANES + h, base + term] = 1.0
            pk[term * LANES + h, base + 3 + term] = -1.0
        cq[0, base + 3:base + 6] = 1.0
        ck[0, base:base + 3] = 1.0
    return jnp.asarray(pq, BF), jnp.asarray(cq), jnp.asarray(pk, BF), jnp.asarray(ck)


def shared_kv_t(h, g, wkv_t, wk, wf_t, bf_col, wf, bf_row, place, *, nb, seq_len, tm=512):
    t, d = h.shape
    lb = t // nb
    tm = min(tm, lb)
    f2 = wkv_t.shape[0]
    width = f2 // 2
    tiles = lb // tm
    body = functools.partial(_kvf_body, seq_len=seq_len, tm=tm, width=width)
    fix = lambda b, i: (0, 0)
    tok = lambda b, i: (b * tiles + i, 0)
    feat = lambda b, i: (b, 0, i)
    return pl.pallas_call(
        body,
        grid=(nb, tiles),
        in_specs=[pl.BlockSpec((tm, d), tok),
                  pl.BlockSpec((1, d), fix),
                  pl.BlockSpec((f2, d), fix),
                  pl.BlockSpec((d, width), fix),
                  pl.BlockSpec((HEAD_PAD, d), fix),
                  pl.BlockSpec((HEAD_PAD, 1), fix),
                  pl.BlockSpec((d, LANES), fix),
                  pl.BlockSpec((1, LANES), fix),
                  pl.BlockSpec((3 * LANES, width), fix),
                  pl.BlockSpec((1, width), fix),
                  pl.BlockSpec((3 * LANES, width), fix),
                  pl.BlockSpec((1, width), fix)],
        out_specs=[pl.BlockSpec((1, width, tm), feat),
                   pl.BlockSpec((1, width, tm), feat),
                   pl.BlockSpec((1, width, tm), feat),
                   pl.BlockSpec((tm, width), tok),
                   pl.BlockSpec((1, HEAD_PAD, tm), feat),
                   pl.BlockSpec((tm, LANES), tok),
                   pl.BlockSpec((tm, width), tok),
                   pl.BlockSpec((tm, width), tok)],
        out_shape=[jax.ShapeDtypeStruct((nb, width, lb), F32),
                   jax.ShapeDtypeStruct((nb, width, lb), F32),
                   jax.ShapeDtypeStruct((nb, width, lb), BF),
                   jax.ShapeDtypeStruct((t, width), BF),
                   jax.ShapeDtypeStruct((nb, HEAD_PAD, lb), F32),
                   jax.ShapeDtypeStruct((t, LANES), F32),
                   jax.ShapeDtypeStruct((t, width), BF),
                   jax.ShapeDtypeStruct((t, width), BF)],
        scratch_shapes=[pltpu.VMEM((1, LANES), F32)],
        compiler_params=_params("parallel", "arbitrary"),
        name="shared_kv",
    )(h, g.reshape(1, d), wkv_t, wk, wf_t, bf_col, wf, bf_row, *place)


def _fox_body(q_ref, aq_ref, k_ref, ak_ref, vt_ref, o_ref, kcat_scr, vcat_scr, m_scr, acc_scr, st_scr, *, tq, dh):
    i = pl.program_id(2)
    per = LANES // dh

    @pl.when(i == 0)
    def _():
        kcat_scr[:, :LANES] = k_ref[0]
        kcat_scr[:, LANES:] = ak_ref[0]
        for e in range(per):
            vcat_scr[e, :dh, :] = vt_ref[0, e * dh:(e + 1) * dh, :]
            vcat_scr[e, dh:, :] = jnp.ones((vcat_scr.shape[1] - dh, vcat_scr.shape[2]), BF)

    qcat_t = jnp.concatenate([(q_ref[0] * (dh ** -0.5 * LOG2E)).T.astype(BF),
                              aq_ref[0].astype(F32).T.astype(BF)], axis=0)
    rows_per = min(FOX_ROWS, tq)
    nslice = tq // rows_per
    frow = lax.broadcasted_iota(jnp.int32, qcat_t.shape, 0)
    key_i = lax.broadcasted_iota(jnp.int32, (rows_per, tq), 0)
    qry_i = lax.broadcasted_iota(jnp.int32, (rows_per, tq), 1)
    qes = []
    for e in range(per):
        sel = ((frow >= e * dh) & (frow < (e + 1) * dh)) | (
            (frow >= LANES + e * SUBLANES) & (frow < LANES + (e + 1) * SUBLANES))
        qes.append(jnp.where(sel, qcat_t, jnp.zeros_like(qcat_t)))
    m_scr[...] = jnp.full(m_scr.shape, NEG_INF, F32)
    acc_scr[...] = jnp.zeros(acc_scr.shape, F32)

    def block_rows(j):
        return pl.ds(pl.multiple_of(j * tq, tq), tq)

    def scores(j, slot, e):
        st_scr[slot * per + e] = _dot(kcat_scr[block_rows(j), :], qes[e])

    def attend(j, slot, e, masked):
        def rows(c):
            st = st_scr[slot * per + e, c * rows_per:(c + 1) * rows_per, :]
            if masked:
                st = jnp.where(key_i + c * rows_per <= qry_i, st, NEG_INF)
            return st
        m_prev = m_scr[e:e + 1, :]
        m_new = m_prev
        for c in range(nslice):
            m_new = jnp.maximum(m_new, jnp.max(rows(c), axis=0, keepdims=True))
        alpha = jnp.exp2(m_prev - m_new)
        pv = None
        for c in range(nslice):
            ks = pl.ds(pl.multiple_of(j * tq + c * rows_per, rows_per), rows_per)
            d = _dot(vcat_scr[e, :, ks], jnp.exp2(rows(c) - m_new).astype(BF))
            pv = d if pv is None else pv + d
        acc_scr[e] = alpha * acc_scr[e] + pv
        m_scr[e:e + 1, :] = m_new

    for e in range(per):
        scores(0, 0, e)

    def pair(t, carry):
        for e in range(per):
            scores(2 * t + 1, 1, e)
            attend(2 * t, 0, e, False)
        for e in range(per):
            scores(2 * t + 2, 0, e)
            attend(2 * t + 1, 1, e, False)
        return carry

    lax.fori_loop(0, i // 2, pair, 0)

    @pl.when(i % 2 == 0)
    def _():
        for e in range(per):
            attend(i, 0, e, True)

    @pl.when(i % 2 == 1)
    def _():
        for e in range(per):
            scores(i, 1, e)
            attend(i - 1, 0, e, False)
        for e in range(per):
            attend(i, 1, e, True)

    out_t = jnp.concatenate([acc_scr[e, :dh, :] / acc_scr[e, dh:dh + 1, :] for e in range(per)], axis=0)
    o_ref[0] = out_t.T


def fox_prompt(z, aug_q, k_tok, aug_k, vt_b, *, dh, width, tq=1024):
    n, l, _ = z.shape
    nblk = width // LANES
    tq = min(tq, l)
    body = functools.partial(_fox_body, tq=tq, dh=dh)
    qblk = pl.BlockSpec((1, tq, LANES), lambda b, j, i: (b, i, j))
    seq = pl.BlockSpec((1, l, LANES), lambda b, j, i: (b, 0, j))
    return pl.pallas_call(
        body,
        grid=(n, nblk, l // tq),
        in_specs=[qblk, qblk, seq, seq,
                  pl.BlockSpec((1, LANES, l), lambda b, j, i: (b, j, 0))],
        out_specs=qblk,
        out_shape=jax.ShapeDtypeStruct((n, l, width), F32),
        scratch_shapes=[pltpu.VMEM((l, 2 * LANES), BF), pltpu.VMEM((LANES // dh, dh + HEAD_PAD, l), BF),
                        pltpu.VMEM((LANES // dh, tq), F32), pltpu.VMEM((LANES // dh, dh + HEAD_PAD, tq), F32),
                        pltpu.VMEM((2 * (LANES // dh), tq, tq), F32)],
        compiler_params=_params("parallel", "parallel", "arbitrary"),
        name="fox_prompt",
    )(z, aug_q, k_tok, aug_k, vt_b)


def _logf_pages_body(x_ref, o_ref):
    heads, pb, _ = x_ref.shape
    for h in range(HEAD_PAD):
        rows = x_ref[h] if h < heads else jnp.zeros(x_ref.shape[1:], F32)
        o_ref[pl.ds(h, pb, stride=HEAD_PAD), :] = rows


def logf_page_major(logf_t, *, pages_per_step=256):
    heads, n_phys, page = logf_t.shape
    pb = math.gcd(pages_per_step, n_phys)
    out = pl.pallas_call(
        _logf_pages_body,
        grid=(n_phys // pb,),
        in_specs=[pl.BlockSpec((heads, pb, page), lambda i: (0, i, 0))],
        out_specs=pl.BlockSpec((pb * HEAD_PAD, page), lambda i: (i, 0)),
        out_shape=jax.ShapeDtypeStruct((n_phys * HEAD_PAD, page), F32),
        compiler_params=_params("parallel"),
        name="logf_page_major",
    )(logf_t)
    return out.reshape(n_phys, HEAD_PAD, page)


def _decode_body(pt_ref, q_ref, kn_ref, vn_ref, bn_ref, *refs, g, heads, lq):
    k_pages = refs[:g]
    v_pages = refs[g:2 * g]
    f_pages = refs[2 * g:3 * g]
    o_ref = refs[3 * g]
    m_scr, l_scr, acc_scr, carry_scr, qbd_scr = refs[3 * g + 1:]
    j = pl.program_id(1)
    rows, width = qbd_scr.shape
    dh = width // heads
    own = (lax.broadcasted_iota(jnp.int32, (rows, width), 0) // lq
           == lax.broadcasted_iota(jnp.int32, (rows, width), 1) // dh)

    @pl.when(j == 0)
    def _():
        q = jnp.concatenate([q_ref[0] * (dh ** -0.5)] * heads, axis=0)
        qbd_scr[...] = jnp.where(own, q, 0.0).astype(BF)
        s = _dot(qbd_scr[...], kn_ref[0]) + bn_ref[0]
        m = jnp.max(s, axis=-1, keepdims=True)
        p = jnp.exp(s - m)
        m_scr[...] = m
        l_scr[...] = jnp.sum(p, axis=-1, keepdims=True)
        acc_scr[...] = _dot_nt(p.astype(BF), vn_ref[0])
        carry_scr[...] = jnp.zeros(carry_scr.shape, F32)

    qbd = qbd_scr[...]
    src = lax.broadcasted_iota(jnp.int32, (LANES, 2 * LANES), 0)
    dst = lax.broadcasted_iota(jnp.int32, (LANES, 2 * LANES), 1)
    w = jnp.where((dst >= LANES) | (src > dst), 1.0, 0.0).astype(BF)
    terms = jnp.concatenate([t for i in range(g) for t in _split3(f_pages[i][0])], axis=0)
    r_all = _dot(terms, w)
    carry = carry_scr[...]
    decay = [None] * g
    for i in reversed(range(g)):
        base = i * 3 * HEAD_PAD
        r = (r_all[base:base + HEAD_PAD] + r_all[base + HEAD_PAD:base + 2 * HEAD_PAD]
             + r_all[base + 2 * HEAD_PAD:base + 3 * HEAD_PAD])
        decay[i] = r[:, :LANES] + carry
        carry = carry + r[:, LANES:]
    carry_scr[...] = carry
    decay = jnp.concatenate(decay, axis=-1)

    s = jnp.concatenate([_dot(qbd, k_pages[i][0].astype(BF)) for i in range(g)], axis=-1)
    rows, cols = s.shape
    s = (s.reshape(heads, lq, cols) + decay[:heads, None, :]).reshape(rows, cols)
    m_prev = m_scr[...]
    m_new = jnp.maximum(m_prev, jnp.max(s, axis=-1, keepdims=True))
    alpha = jnp.exp(m_prev - m_new)
    p = jnp.exp(s - m_new).astype(BF)
    l_scr[...] = alpha * l_scr[...] + jnp.sum(p.astype(F32), axis=-1, keepdims=True)
    acc = alpha * acc_scr[...]
    page = cols // g
    for i in range(g):
        acc = acc + _dot_nt(p[:, i * page:(i + 1) * page], v_pages[i][0].astype(BF))
    acc_scr[...] = acc
    m_scr[...] = m_new

    @pl.when(j == pl.num_programs(1) - 1)
    def _():
        o = jnp.where(own, acc_scr[...] / l_scr[...], 0.0)
        out = o[0:lq]
        for h in range(1, heads):
            out = out + o[h * lq:(h + 1) * lq]
        o_ref[0] = out


def fox_decode(page_table, z, kn_t, vn_t, bias_new, k_pages_t, v_pages_t, logf_pages, *, heads, width):
    nb, npg = page_table.shape
    lq = z.shape[1]
    rows = heads * lq
    page = k_pages_t.shape[-1]
    g = min(DEC_PAGES_PER_STEP, npg)
    steps = npg // g

    def page_spec(i, sublanes):
        return pl.BlockSpec((1, sublanes, page), lambda b, j, pt: (pt[b, (steps - 1 - j) * g + i], 0, 0))

    fixed = lambda b, j, pt: (b, 0, 0)
    grid_spec = pltpu.PrefetchScalarGridSpec(
        num_scalar_prefetch=1,
        grid=(nb, steps),
        in_specs=[pl.BlockSpec((1, lq, width), fixed),
                  pl.BlockSpec((1, width, LANES), fixed),
                  pl.BlockSpec((1, width, LANES), fixed),
                  pl.BlockSpec((1, rows, LANES), fixed)]
                 + [page_spec(i, width) for i in range(g)] + [page_spec(i, width) for i in range(g)]
                 + [page_spec(i, HEAD_PAD) for i in range(g)],
        out_specs=pl.BlockSpec((1, lq, width), fixed),
        scratch_shapes=[pltpu.VMEM((rows, 1), F32), pltpu.VMEM((rows, 1), F32),
                        pltpu.VMEM((rows, width), F32), pltpu.VMEM((HEAD_PAD, LANES), F32),
                        pltpu.VMEM((rows, width), BF)],
    )
    return pl.pallas_call(
        functools.partial(_decode_body, g=g, heads=heads, lq=lq),
        grid_spec=grid_spec,
        out_shape=jax.ShapeDtypeStruct((nb, lq, width), F32),
        compiler_params=_params("parallel", "arbitrary"),
        name="fox_decode",
    )(page_table, z, kn_t, vn_t, bias_new, *([k_pages_t] * g), *([v_pages_t] * g), *([logf_pages] * g))


def kernel(x_prompt, x_sample, state_s5_re, state_s5_im, cache_mem_k, cache_mem_v, cache_k, cache_v, cache_logf, page_table, mem_prompt, norm_mix, norm_mlp, w_in, w_out, w_up, w_down, w_mem_kv, s5_a_re, s5_a_im, s5_log_step, s5_b_re, s5_b_im, s5_c_re, s5_c_im, s5_d, s5_w_glu, s5_b_glu, norm_kv, w_kv, w_f, b_f, norm_final):
    n_p, seq, d_model = x_prompt.shape
    n_s, dec_seq, _ = x_sample.shape
    depth = w_in.shape[0]
    n_a = s5_a_re.shape[0]
    assert depth == 2 and n_a == 1, "layer pattern: one S5 layer then one FoX layer"
    mem_heads, mem_dh = cache_mem_k.shape[3:]
    d_mem = mem_heads * mem_dh
    width = d_model - d_mem
    fox_heads, fox_dh = cache_k.shape[2:]
    page = cache_k.shape[1]
    group_ch = s5_b_re.shape[-1]
    s5_state = s5_a_re.shape[-1]
    per = LANES // group_ch
    mem_col = width // d_mem
    assert width % d_mem == 0 and width % LANES == 0 and LANES % fox_dh == 0 and LANES % group_ch == 0
    assert fox_heads <= HEAD_PAD and page == LANES
    assert dec_seq == S5_CHUNK and seq % S5_CHUNK == 0
    assert S5_CHUNK * group_ch == LANES and 2 * s5_state == LANES
    assert (seq // S5_CHUNK) % min(S5_ROWS_PER_TILE, seq // S5_CHUNK) == 0

    w_in_b = w_in.astype(BF)
    w_out_b = w_out.astype(BF)
    w_up_b = w_up.astype(BF)
    w_down_b = w_down.astype(BF)
    w_glu_b = s5_w_glu.astype(BF)
    g_mix = norm_mix.reshape(depth, 1, d_model)
    g_mlp = norm_mlp.reshape(depth, 1, d_model)
    w_memkv_t = w_mem_kv.transpose(0, 2, 1).astype(BF)
    wkv_t = w_kv.T.astype(BF)
    wk = w_kv[:, :width].astype(BF)
    wf_t = jnp.pad(w_f.T, ((0, HEAD_PAD - fox_heads), (0, 0))).astype(BF)
    bf_col = jnp.pad(b_f, (0, HEAD_PAD - fox_heads)).reshape(HEAD_PAD, 1)
    wf = jnp.pad(w_f, ((0, 0), (0, LANES - fox_heads))).astype(BF)
    bf_row = jnp.pad(b_f, (0, LANES - fox_heads)).reshape(1, LANES)
    place = fox_bias_placement(fox_heads, fox_dh, width)

    def mem_t(c):
        return c.transpose(0, 1, 3, 4, 2).reshape(c.shape[0], c.shape[1], d_mem, c.shape[2])

    k_pages_t = cache_k.transpose(0, 2, 3, 1).reshape(cache_k.shape[0], width, page)
    v_pages_t = cache_v.transpose(0, 2, 3, 1).reshape(cache_v.shape[0], width, page)
    logf_pages = logf_page_major(cache_logf.transpose(2, 0, 1))

    s5_ops = s5_operators(s5_a_re[0], s5_a_im[0], s5_log_step[0], s5_b_re[0], s5_b_im[0],
                          s5_c_re[0], s5_c_im[0], s5_d[0])

    def mem_attend(z, mem, layer):
        kt, vt, vblk = mem
        n, l, _ = z.shape
        return memory_attend(z, kt, vt, layer, heads=mem_heads, f=d_mem, col_block=mem_col,
                             v_row_block=vblk).reshape(n * l, d_mem)

    def layer0(x, z_view, h0, mem, per_row_state):
        n, l, _ = x.shape
        h = x.reshape(n * l, d_model)
        z = rms_matmul(h, g_mix, w_in_b, 0).reshape(n, l, d_model)
        y_mem = mem_attend(z, mem, 0)
        y_s5, hend = s5_core(z.reshape(z_view), s5_ops, h0, width=width, per_row_state=per_row_state)
        h = layer_tail(h, y_s5.reshape(n * l, width), y_mem, w_out_b, g_mlp, w_up_b, w_down_b, 0,
                       glu=(w_glu_b[0], s5_b_glu[0]))
        return h, hend

    def layer1_front(h, n, l, mem, nb, seq_len):
        z = rms_matmul(h, g_mix, w_in_b, 1).reshape(n, l, d_model)
        y_mem = mem_attend(z, mem, 1)
        kv = shared_kv_t(h, norm_kv, wkv_t, wk, wf_t, bf_col, wf, bf_row, place, nb=nb, seq_len=seq_len)
        return z, y_mem, kv

    def layer1_back(h, y_fox, y_mem):
        return layer_tail(h, y_fox, y_mem, w_out_b, g_mlp, w_up_b, w_down_b, 1, final_g=norm_final)

    memkv_t = memory_kv_t(mem_prompt, w_memkv_t)
    m_tok = mem_prompt.shape[1]
    pmk_t = memkv_t[:, :, :d_mem]
    pmv_t = memkv_t[:, :, d_mem:]
    to_mem = lambda a: a.reshape(depth, n_p, mem_heads, mem_dh, m_tok).transpose(0, 1, 4, 2, 3)
    p_mem_k = to_mem(pmk_t)
    p_mem_v = to_mem(pmv_t)

    sw = 2 * per * s5_state
    zeros = jnp.zeros((width // LANES, n_p, 1, sw), F32)
    p_mem = (memkv_t, memkv_t, 1)
    h, p_hend = layer0(x_prompt, (n_p, seq, d_model), zeros, p_mem, False)
    p_hr, p_hi = s5_state_out(p_hend[:, :, 0], per, s5_state)
    z, y_mem, (kt, vt, vtb, ktok, lft, _, aq, ak) = layer1_front(h, n_p, seq, p_mem, n_p, seq)
    tokmaj = lambda a: a.reshape(n_p, seq, width)
    y_fox = fox_prompt(z, tokmaj(aq), tokmaj(ktok), tokmaj(ak), vtb, dh=fox_dh, width=width)
    y_prompt = layer1_back(h, y_fox.reshape(n_p * seq, width), y_mem).reshape(n_p, seq, d_model)
    to_heads = lambda a, n, l: a.reshape(n, fox_heads, fox_dh, l).transpose(0, 3, 1, 2)
    p_k = to_heads(kt, n_p, seq)
    p_v = to_heads(vt, n_p, seq)
    p_logf = lft[:, :fox_heads].transpose(0, 2, 1)

    s_mem = (mem_t(cache_mem_k), mem_t(cache_mem_v), 0)
    tok = n_s * dec_seq
    s_h0 = s5_state_in(state_s5_re[0], state_s5_im[0], per)[:, None]
    hs, s_hend = layer0(x_sample, (1, tok, d_model), s_h0, s_mem, True)
    s_hr, s_hi = s5_state_out(s_hend[:, 0], per, s5_state)
    zs, ys_mem, (kts, vts, _, _, lfts, cums, _, _) = layer1_front(hs, n_s, dec_seq, s_mem, 1, dec_seq)
    new_heads = lambda a: a[0].reshape(fox_heads, fox_dh, n_s, dec_seq).transpose(2, 3, 0, 1)
    s_k = new_heads(kts)
    s_v = new_heads(vts)
    s_logf = lfts[0, :fox_heads].reshape(fox_heads, n_s, dec_seq).transpose(1, 2, 0)

    new_t = jnp.stack([kts[0], vts[0]]).reshape(2, width, n_s, dec_seq).transpose(0, 2, 1, 3)
    new_t = jnp.pad(new_t, ((0, 0), (0, 0), (0, 0), (0, LANES - dec_seq))).astype(BF)
    c_new = cums[:, :fox_heads].reshape(n_s, dec_seq, fox_heads).transpose(0, 2, 1)
    qi = jnp.arange(dec_seq)[:, None]
    tp = jnp.arange(LANES)[None, :]
    c_pad = jnp.pad(c_new, ((0, 0), (0, 0), (0, LANES - dec_seq)))
    bias_new = jnp.where((tp <= qi)[None, None], -c_pad[:, :, None, :], NEG_INF).reshape(
        n_s, fox_heads * dec_seq, LANES)
    ys_fox = fox_decode(page_table, zs, new_t[0], new_t[1], bias_new, k_pages_t, v_pages_t, logf_pages,
                        heads=fox_heads, width=width).reshape(tok, width)
    y_sample = layer1_back(hs, ys_fox, ys_mem).reshape(n_s, dec_seq, d_model)

    return (y_prompt, y_sample, p_hr[None], p_hi[None], p_mem_k, p_mem_v, p_k, p_v, p_logf,
            s_hr[None], s_hi[None], s_k, s_v, s_logf)
```

```python
import functools
import math

import jax
import jax.numpy as jnp
import numpy as np
from jax import lax
from jax.experimental import pallas as pl
from jax.experimental.pallas import tpu as pltpu

BF = jnp.bfloat16
F32 = jnp.float32
RMS_EPS = 1e-6
NEG_INF = -1e30
LOG2E = math.log2(math.e)
V7X_VMEM_BYTES = 64 * 1024 * 1024
VMEM_LIMIT = V7X_VMEM_BYTES - 8 * 1024 * 1024
LANES = 128
SUBLANES = 8
S5_CHUNK = SUBLANES
S5_ROWS_PER_TILE = 256
HEAD_PAD = 16
DEC_PAGES_PER_STEP = 16
MEM_SEQS_PER_STEP = 8
FOX_ROWS = 256

NT_DIMS = (((1,), (1,)), ((), ()))


def _params(*sem):
    return pltpu.CompilerParams(dimension_semantics=sem, vmem_limit_bytes=VMEM_LIMIT)


def _dot(a, b):
    return jnp.dot(a, b, preferred_element_type=F32)


def _dot_nt(a, b):
    return lax.dot_general(a, b, NT_DIMS, preferred_element_type=F32)


def _rms(x, g):
    ms = jnp.mean(x * x, axis=-1, keepdims=True)
    return x * lax.rsqrt(ms + RMS_EPS) * g


def _rms_matmul_body(x_ref, g_ref, w_ref, o_ref):
    xn = _rms(x_ref[...], g_ref[0])
    o_ref[...] = _dot(xn.astype(BF), w_ref[0])


def rms_matmul(x, g, w, layer, *, tm=1024):
    t, d = x.shape
    n = w.shape[2]
    tm = min(tm, t)
    return pl.pallas_call(
        _rms_matmul_body,
        grid=(t // tm,),
        in_specs=[pl.BlockSpec((tm, d), lambda i: (i, 0)),
                  pl.BlockSpec((1, 1, d), lambda i: (layer, 0, 0)),
                  pl.BlockSpec((1, d, n), lambda i: (layer, 0, 0))],
        out_specs=pl.BlockSpec((tm, n), lambda i: (i, 0)),
        out_shape=jax.ShapeDtypeStruct((t, n), F32),
        compiler_params=_params("parallel"),
        name="rms_matmul",
    )(x, g, w)


def _memkv_body(w_ref, m_ref, o_ref):
    o_ref[0, 0] = _dot_nt(w_ref[0], m_ref[0].astype(BF))


def memory_kv_t(mem, w_t):
    nb, m, d = mem.shape
    depth, f, _ = w_t.shape
    return pl.pallas_call(
        _memkv_body,
        grid=(depth, nb),
        in_specs=[pl.BlockSpec((1, f, d), lambda l, b: (l, 0, 0)),
                  pl.BlockSpec((1, m, d), lambda l, b: (b, 0, 0))],
        out_specs=pl.BlockSpec((1, 1, f, m), lambda l, b: (l, b, 0, 0)),
        out_shape=jax.ShapeDtypeStruct((depth, nb, f, m), F32),
        compiler_params=_params("parallel", "parallel"),
        name="memory_kv",
    )(w_t, mem)


def _memattn_body(q_ref, kt_ref, vt_ref, o_ref, *, heads, dh):
    for sq in range(q_ref.shape[0]):
        q = q_ref[sq] * (dh ** -0.5)
        kt = kt_ref[0, sq].astype(BF)
        vt = vt_ref[0, sq].astype(BF)
        lane = lax.broadcasted_iota(jnp.int32, q.shape, 1)
        out = jnp.zeros(q.shape, F32)
        for h in range(heads):
            sel = (lane >= h * dh) & (lane < (h + 1) * dh)
            s = _dot(jnp.where(sel, q, 0.0).astype(BF), kt)
            m = jnp.max(s, axis=-1, keepdims=True)
            p = jnp.exp(s - m)
            l = jnp.sum(p, axis=-1, keepdims=True)
            o = _dot_nt((p / l).astype(BF), vt)
            out = jnp.where(sel, o, out)
        o_ref[sq] = out


def memory_attend(z, kt, vt, layer, *, heads, f, col_block, k_row_block=0, v_row_block=0, tm=512):
    n, l, _ = z.shape
    m = kt.shape[-1]
    tm = min(tm, l)
    seqs = MEM_SEQS_PER_STEP if (l == tm and n % MEM_SEQS_PER_STEP == 0) else 1
    body = functools.partial(_memattn_body, heads=heads, dh=f // heads)
    return pl.pallas_call(
        body,
        grid=(n // seqs, l // tm),
        in_specs=[pl.BlockSpec((seqs, tm, f), lambda b, i: (b, i, col_block)),
                  pl.BlockSpec((1, seqs, f, m), lambda b, i: (layer, b, k_row_block, 0)),
                  pl.BlockSpec((1, seqs, f, m), lambda b, i: (layer, b, v_row_block, 0))],
        out_specs=pl.BlockSpec((seqs, tm, f), lambda b, i: (b, i, 0)),
        out_shape=jax.ShapeDtypeStruct((n, l, f), F32),
        compiler_params=_params("parallel", "parallel"),
        name="memory_attend",
    )(z, kt, vt)


def _cmul(a, h):
    return a[0:1, :] * h + a[1:2, :] * pltpu.roll(h, h.shape[-1] // 2, axis=1)


def _s5_expand(src_ref, dst_ref, *, row_inner, col_inner):
    per = src_ref.shape[1] // LANES
    wide = per * LANES
    r = lax.broadcasted_iota(jnp.int32, (LANES, wide), 0)
    q = lax.broadcasted_iota(jnp.int32, (LANES, wide), 1)
    tile = jnp.where((r // col_inner == q // (per * col_inner)) & (r % col_inner == q % col_inner),
                     1.0, 0.0).astype(BF)
    spread = _dot(src_ref[0], tile).astype(BF)
    qg = (lax.broadcasted_iota(jnp.int32, (row_inner, wide), 1) // col_inner) % per
    for g in range(per):
        for a in range(LANES // row_inner):
            blk = spread[g * LANES + a * row_inner:g * LANES + (a + 1) * row_inner, :]
            dst_ref[pl.ds((a * per + g) * row_inner, row_inner), :] = jnp.where(qg == g, blk, jnp.zeros_like(blk))


def _s5_body(z_ref, kc_ref, wc_ref, oc_ref, apow_ref, h0_ref, y_ref, hend_ref, carry_ref,
             kx_scr, wst_scr, wout_scr, *, rt, per_row_state, group_ch, state):
    i = pl.program_id(2)
    t_len = S5_CHUNK

    @pl.when((pl.program_id(1) == 0) & (i == 0))
    def _():
        _s5_expand(kc_ref, kx_scr, row_inner=group_ch, col_inner=group_ch)
        _s5_expand(wc_ref, wst_scr, row_inner=group_ch, col_inner=state)
        _s5_expand(oc_ref, wout_scr, row_inner=state, col_inner=group_ch)

    lhs = jnp.concatenate([z_ref[0, pl.ds(t, rt, stride=t_len), :].astype(BF) for t in range(t_len)],
                          axis=-1)
    y = _dot(lhs, kx_scr[...])
    hs = _dot(lhs, wst_scr[...])
    a_chunk = apow_ref[0, 0]
    if per_row_state:
        h_in = h0_ref[0, 0]
        hs = hs + _cmul(a_chunk, h_in)
        hend_ref[0, 0] = hs
    else:
        @pl.when(i == 0)
        def _():
            carry_ref[...] = h0_ref[0, 0]
        h0 = carry_ref[...]
        rid = lax.broadcasted_iota(jnp.int32, hs.shape, 0)
        hs = hs + jnp.where(rid == 0, _cmul(a_chunk, h0), 0.0)
        step, k = 1, 0
        while step < rt:
            shifted = jnp.where(rid >= step, pltpu.roll(hs, step, axis=0), 0.0)
            hs = hs + _cmul(apow_ref[0, k], shifted)
            step *= 2
            k += 1
        h_in = jnp.where(rid == 0, h0, pltpu.roll(hs, 1, axis=0))
        carry_ref[...] = hs[rt - 1:rt, :]
        hend_ref[0, 0] = hs[rt - 1:rt, :]
    y = y + _dot(h_in.astype(BF), wout_scr[...])
    for t in range(t_len):
        y_ref[0, pl.ds(t, rt, stride=t_len), :] = y[:, t * LANES:(t + 1) * LANES]


def s5_core(z, ops, h0, *, width, per_row_state):
    kc, wc, oc, apow = ops
    nb, l, _ = z.shape
    nblk = width // LANES
    sw = kc.shape[1]
    state = sw // 2 // (sw // LANES)
    rows_total = l // S5_CHUNK
    rt = rows_total if per_row_state else min(S5_ROWS_PER_TILE, rows_total)
    tiles = rows_total // rt
    rs = h0.shape[2]
    nk = apow.shape[1]
    body = functools.partial(_s5_body, rt=rt, per_row_state=per_row_state,
                             group_ch=LANES // (sw // LANES), state=state)
    compact = pl.BlockSpec((1, sw, LANES), lambda j, b, i: (j, 0, 0))
    return pl.pallas_call(
        body,
        grid=(nblk, nb, tiles),
        in_specs=[pl.BlockSpec((1, rt * S5_CHUNK, LANES), lambda j, b, i: (b, i, j)),
                  compact, compact, compact,
                  pl.BlockSpec((1, nk, 2, sw), lambda j, b, i: (j, 0, 0, 0)),
                  pl.BlockSpec((1, 1, rs, sw), lambda j, b, i: (j, b, 0, 0))],
        out_specs=[pl.BlockSpec((1, rt * S5_CHUNK, LANES), lambda j, b, i: (b, i, j)),
                   pl.BlockSpec((1, 1, rs, sw), lambda j, b, i: (j, b, 0, 0))],
        out_shape=[jax.ShapeDtypeStruct((nb, l, width), F32),
                   jax.ShapeDtypeStruct((nblk, nb, rs, sw), F32)],
        scratch_shapes=[pltpu.VMEM((1, sw), F32), pltpu.VMEM((sw, sw), BF), pltpu.VMEM((sw, sw), BF),
                        pltpu.VMEM((sw, sw), BF)],
        compiler_params=_params("arbitrary", "arbitrary", "arbitrary"),
        name="s5_core",
    )(z, kc, wc, oc, apow, h0)


def s5_operators(a_re, a_im, log_step, b_re, b_im, c_re, c_im, d):
    chunk = S5_CHUNK
    g, p = a_re.shape
    c = b_re.shape[-1]
    per = LANES // c
    nblk = g // per
    dt = jnp.exp(log_step.astype(F32))[:, None]
    a_re = a_re.astype(F32)
    a_im = a_im.astype(F32)
    lam_re = a_re * dt
    lam_im = a_im * dt
    mag = jnp.exp(lam_re)
    ab_re = mag * jnp.cos(lam_im)
    ab_im = mag * jnp.sin(lam_im)
    den = a_re * a_re + a_im * a_im
    nr = ab_re - 1.0
    ni = ab_im
    cr = (nr * a_re + ni * a_im) / den
    ci = (ni * a_re - nr * a_im) / den
    bb_re = cr[..., None] * b_re - ci[..., None] * b_im
    bb_im = cr[..., None] * b_im + ci[..., None] * b_re

    def power(tau):
        tau = tau.astype(F32)[None, :, None]
        m = jnp.exp(lam_re[:, None, :] * tau)
        ph = lam_im[:, None, :] * tau
        return m * jnp.cos(ph), m * jnp.sin(ph)

    pw_re, pw_im = power(jnp.arange(chunk + 1))
    e_re = pw_re[..., None] * bb_re[:, None] - pw_im[..., None] * bb_im[:, None]
    e_im = pw_re[..., None] * bb_im[:, None] + pw_im[..., None] * bb_re[:, None]
    et_re = e_re[:, :chunk].transpose(0, 1, 3, 2)[:, :, None]
    et_im = e_im[:, :chunk].transpose(0, 1, 3, 2)[:, :, None]
    ktau = jnp.sum(c_re[:, None, :, None, :] * et_re - c_im[:, None, :, None, :] * et_im, axis=-1)
    ktau = ktau.at[:, 0].add(d[:, :, None] * jnp.eye(c, dtype=F32)[None])
    s_idx = jnp.arange(chunk)[:, None]
    t_idx = jnp.arange(chunk)[None, :]
    lag = t_idx - s_idx
    toe = jnp.where((lag >= 0)[None, :, :, None, None], ktau[:, jnp.maximum(lag, 0)], 0.0)
    kc = toe.transpose(0, 1, 4, 2, 3).reshape(nblk, per * chunk * c, chunk * c)

    rev = chunk - 1 - jnp.arange(chunk)
    w_c = jnp.stack([e_re[:, rev], e_im[:, rev]], axis=2)
    wc = w_c.transpose(0, 1, 4, 2, 3).reshape(nblk, per * chunk * c, 2 * p)

    q_re = pw_re[:, 1:chunk + 1]
    q_im = pw_im[:, 1:chunk + 1]
    wo_re = c_re[:, None] * q_re[:, :, None, :] - c_im[:, None] * q_im[:, :, None, :]
    wo_im = -(c_re[:, None] * q_im[:, :, None, :] + c_im[:, None] * q_re[:, :, None, :])
    o_c = jnp.stack([wo_re, wo_im], axis=1)
    oc = o_c.transpose(0, 1, 4, 2, 3).reshape(nblk, per * 2 * p, chunk * c)

    nk = max(1, int(math.log2(S5_ROWS_PER_TILE)))
    sc_re, sc_im = power(chunk * (2 ** jnp.arange(nk)))
    lay = lambda x: x.reshape(nblk, per, nk, p).transpose(0, 2, 1, 3).reshape(nblk, nk, per * p)
    sc_re, sc_im = lay(sc_re), lay(sc_im)
    apow = jnp.stack([jnp.concatenate([sc_re, sc_re], -1), jnp.concatenate([-sc_im, sc_im], -1)], axis=2)
    return kc.astype(BF), wc.astype(BF), oc.astype(BF), apow


def s5_state_in(h_re, h_im, per):
    n, g, p = h_re.shape
    f = lambda x: x.reshape(n, g // per, per * p)
    return jnp.concatenate([f(h_re), f(h_im)], axis=-1).transpose(1, 0, 2)


def s5_state_out(h, per, p):
    nblk, n, w2 = h.shape
    h = h.transpose(1, 0, 2)
    f = lambda x: x.reshape(n, nblk * per, p)
    return f(h[..., :w2 // 2]), f(h[..., w2 // 2:])


def _gelu_tanh(x):
    return 0.5 * x * (1.0 + jnp.tanh(math.sqrt(2.0 / math.pi) * (x + 0.044715 * (x * x * x))))


def _tail_body(*refs, glu, final_norm, width):
    it = iter(refs)
    h_ref, ym_ref, ymem_ref, wo_ref = next(it), next(it), next(it), next(it)
    if glu:
        wg_ref, bg_ref = next(it), next(it)
    gm_ref, wu_ref, wd_ref = next(it), next(it), next(it)
    if final_norm:
        gf_ref = next(it)
    o_ref, xn_scr = next(it), next(it)
    c = pl.program_id(1)

    @pl.when(c == 0)
    def _():
        ym = ym_ref[...]
        if glu:
            zz = _gelu_tanh(ym)
            gate = jax.nn.sigmoid(_dot(zz.astype(BF), wg_ref[...]) + bg_ref[...])
            ym = zz * gate
        d = wo_ref.shape[1]
        h = (h_ref[...] + _dot(ym.astype(BF), wo_ref[0, 0:width, :])
             + _dot(ymem_ref[...].astype(BF), wo_ref[0, width:d, :]))
        xn_scr[...] = _rms(h, gm_ref[0]).astype(BF)
        o_ref[...] = h

    up = jnp.maximum(_dot(xn_scr[...], wu_ref[0]), 0.0)
    o_ref[...] += _dot((up * up).astype(BF), wd_ref[0])

    if final_norm:
        @pl.when(c == pl.num_programs(1) - 1)
        def _():
            o_ref[...] = _rms(o_ref[...], gf_ref[...])


def layer_tail(h, y_main, y_mem, w_out, g_mlp, w_up, w_down, layer, *, glu=None, final_g=None, tm=1024,
               ff_chunk=2048):
    t, d = h.shape
    width = y_main.shape[1]
    wm = y_mem.shape[1]
    dff = w_up.shape[2]
    tm = min(tm, t)
    ff_chunk = min(ff_chunk, dff)
    row = lambda i, c: (i, 0)
    fix = lambda i, c: (0, 0)
    args = [h, y_main, y_mem, w_out]
    specs = [pl.BlockSpec((tm, d), row), pl.BlockSpec((tm, width), row), pl.BlockSpec((tm, wm), row),
             pl.BlockSpec((1, d, d), lambda i, c: (layer, 0, 0))]
    if glu is not None:
        args += [glu[0], glu[1].reshape(1, width)]
        specs += [pl.BlockSpec((width, width), fix), pl.BlockSpec((1, width), fix)]
    args += [g_mlp, w_up, w_down]
    specs += [pl.BlockSpec((1, 1, d), lambda i, c: (layer, 0, 0)),
              pl.BlockSpec((1, d, ff_chunk), lambda i, c: (layer, 0, c)),
              pl.BlockSpec((1, ff_chunk, d), lambda i, c: (layer, c, 0))]
    if final_g is not None:
        args.append(final_g.reshape(1, d))
        specs.append(pl.BlockSpec((1, d), fix))
    body = functools.partial(_tail_body, glu=glu is not None, final_norm=final_g is not None, width=width)
    return pl.pallas_call(
        body,
        grid=(t // tm, dff // ff_chunk),
        in_specs=specs,
        out_specs=pl.BlockSpec((tm, d), row),
        out_shape=jax.ShapeDtypeStruct((t, d), F32),
        scratch_shapes=[pltpu.VMEM((tm, d), BF)],
        compiler_params=_params("parallel", "arbitrary"),
        name="layer_tail",
    )(*args)


def _log_sigmoid(x):
    return jnp.minimum(x, 0.0) - jnp.log(1.0 + jnp.exp(-jnp.abs(x)))


def _chop(x):
    return lax.bitcast_convert_type(lax.bitcast_convert_type(x, jnp.uint32) & jnp.uint32(0xFFFF0000), F32)


def _split3(x):
    hi = _chop(x)
    mid = _chop(x - hi)
    lo = x - hi - mid
    return hi.astype(BF), mid.astype(BF), lo.astype(BF)


def _kvf_body(x_ref, g_ref, wkvt_ref, wk_ref, wft_ref, bfc_ref, wf_ref, bfr_ref, pq_ref, cq_ref, pk_ref, ck_ref,
              kt_ref, vt_ref, vtb_ref, ktok_ref, lft_ref, cum_ref, aq_ref, ak_ref, carry_ref,
              *, seq_len, tm, width):
    i = pl.program_id(1)
    hn = _rms(x_ref[...], g_ref[...]).astype(BF)
    kv = _dot_nt(wkvt_ref[...], hn)
    kt_ref[0] = kv[:width]
    vt_ref[0] = kv[width:]
    vtb_ref[0] = kv[width:].astype(BF)
    ktok_ref[...] = _dot(hn, wk_ref[...]).astype(BF)
    lft_ref[0] = _log_sigmoid(_dot_nt(wft_ref[...], hn) + bfc_ref[...])
    lf = _log_sigmoid(_dot(hn, wf_ref[...]) + bfr_ref[...])
    dst = lax.broadcasted_iota(jnp.int32, (tm, tm), 0)
    src = lax.broadcasted_iota(jnp.int32, (tm, tm), 1)
    keep = src <= dst
    if seq_len < tm:
        keep = keep & ((src // seq_len) == (dst // seq_len))
    tri = jnp.where(keep, 1.0, 0.0).astype(BF)
    hi, mid, lo = _split3(lf)
    cum = _dot(tri, hi) + _dot(tri, mid) + _dot(tri, lo)
    if seq_len > tm:
        @pl.when(i % (seq_len // tm) == 0)
        def _():
            carry_ref[...] = jnp.zeros(carry_ref.shape, F32)
        cum = cum + carry_ref[...]
        carry_ref[...] = cum[tm - 1:tm, :]
    cum_ref[...] = cum
    parts = jnp.concatenate(_split3(cum * LOG2E), axis=-1)
    aq_ref[...] = (_dot(parts, pq_ref[...]) + cq_ref[...]).astype(BF)
    ak_ref[...] = (_dot(parts, pk_ref[...]) + ck_ref[...]).astype(BF)


def fox_bias_placement(heads, dh, width):
    per = LANES // dh
    pq = np.zeros((3 * LANES, width), np.float32)
    pk = np.zeros((3 * LANES, width), np.float32)
    cq = np.zeros((1, width), np.float32)
    ck = np.zeros((1, width), np.float32)
    for h in range(heads):
        base = (h // per) * LANES + (h % per) * SUBLANES
        for term in range(3):
            pq[term * LANES + h, base + term] = 1.0
            pk[term * LANES + h, base + 3 + term] = -1.0
        cq[0, base + 3:base + 6] = 1.0
        ck[0, base:base + 3] = 1.0
    return jnp.asarray(pq, BF), jnp.asarray(cq), jnp.asarray(pk, BF), jnp.asarray(ck)


def shared_kv_t(h, g, wkv_t, wk, wf_t, bf_col, wf, bf_row, place, *, nb, seq_len, tm=512):
    t, d = h.shape
    lb = t // nb
    tm = min(tm, lb)
    f2 = wkv_t.shape[0]
    width = f2 // 2
    tiles = lb // tm
    body = functools.partial(_kvf_body, seq_len=seq_len, tm=tm, width=width)
    fix = lambda b, i: (0, 0)
    tok = lambda b, i: (b * tiles + i, 0)
    feat = lambda b, i: (b, 0, i)
    return pl.pallas_call(
        body,
        grid=(nb, tiles),
        in_specs=[pl.BlockSpec((tm, d), tok),
                  pl.BlockSpec((1, d), fix),
                  pl.BlockSpec((f2, d), fix),
                  pl.BlockSpec((d, width), fix),
                  pl.BlockSpec((HEAD_PAD, d), fix),
                  pl.BlockSpec((HEAD_PAD, 1), fix),
                  pl.BlockSpec((d, LANES), fix),
                  pl.BlockSpec((1, LANES), fix),
                  pl.BlockSpec((3 * LANES, width), fix),
                  pl.BlockSpec((1, width), fix),
                  pl.BlockSpec((3 * LANES, width), fix),
                  pl.BlockSpec((1, width), fix)],
        out_specs=[pl.BlockSpec((1, width, tm), feat),
                   pl.BlockSpec((1, width, tm), feat),
                   pl.BlockSpec((1, width, tm), feat),
                   pl.BlockSpec((tm, width), tok),
                   pl.BlockSpec((1, HEAD_PAD, tm), feat),
                   pl.BlockSpec((tm, LANES), tok),
                   pl.BlockSpec((tm, width), tok),
                   pl.BlockSpec((tm, width), tok)],
        out_shape=[jax.ShapeDtypeStruct((nb, width, lb), F32),
                   jax.ShapeDtypeStruct((nb, width, lb), F32),
                   jax.ShapeDtypeStruct((nb, width, lb), BF),
                   jax.ShapeDtypeStruct((t, width), BF),
                   jax.ShapeDtypeStruct((nb, HEAD_PAD, lb), F32),
                   jax.ShapeDtypeStruct((t, LANES), F32),
                   jax.ShapeDtypeStruct((t, width), BF),
                   jax.ShapeDtypeStruct((t, width), BF)],
        scratch_shapes=[pltpu.VMEM((1, LANES), F32)],
        compiler_params=_params("parallel", "arbitrary"),
        name="shared_kv",
    )(h, g.reshape(1, d), wkv_t, wk, wf_t, bf_col, wf, bf_row, *place)


def _fox_body(q_ref, aq_ref, k_ref, ak_ref, vt_ref, o_ref, kcat_scr, vcat_scr, m_scr, acc_scr, st_scr, bmax_scr,
              *, tq, dh):
    i = pl.program_id(2)
    per = LANES // dh

    @pl.when(i == 0)
    def _():
        kcat_scr[:, :LANES] = k_ref[0]
        kcat_scr[:, LANES:] = ak_ref[0]
        for e in range(per):
            vcat_scr[e, :dh, :] = vt_ref[0, e * dh:(e + 1) * dh, :]
            vcat_scr[e, dh:, :] = jnp.ones((vcat_scr.shape[1] - dh, vcat_scr.shape[2]), BF)

    qcat_t = jnp.concatenate([(q_ref[0] * (dh ** -0.5 * LOG2E)).T.astype(BF),
                              aq_ref[0].astype(F32).T.astype(BF)], axis=0)
    rows_per = min(FOX_ROWS, tq)
    nslice = tq // rows_per
    frow = lax.broadcasted_iota(jnp.int32, qcat_t.shape, 0)
    key_i = lax.broadcasted_iota(jnp.int32, (rows_per, tq), 0)
    qry_i = lax.broadcasted_iota(jnp.int32, (rows_per, tq), 1)
    qes = []
    for e in range(per):
        sel = ((frow >= e * dh) & (frow < (e + 1) * dh)) | (
            (frow >= LANES + e * SUBLANES) & (frow < LANES + (e + 1) * SUBLANES))
        qes.append(jnp.where(sel, qcat_t, jnp.zeros_like(qcat_t)))
    m_scr[...] = jnp.full(m_scr.shape, NEG_INF, F32)
    acc_scr[...] = jnp.zeros(acc_scr.shape, F32)

    def block_rows(j):
        return pl.ds(pl.multiple_of(j * tq, tq), tq)

    def scores(j, slot, e):
        st = _dot(kcat_scr[block_rows(j), :], qes[e])
        st_scr[slot * per + e] = st
        bmax_scr[slot * per + e:slot * per + e + 1, :] = jnp.max(st, axis=0, keepdims=True)

    def attend(j, slot, e, masked):
        def rows(c):
            st = st_scr[slot * per + e, c * rows_per:(c + 1) * rows_per, :]
            if masked:
                st = jnp.where(key_i + c * rows_per <= qry_i, st, NEG_INF)
            return st
        m_prev = m_scr[e:e + 1, :]
        if masked:
            m_new = m_prev
            for c in range(nslice):
                m_new = jnp.maximum(m_new, jnp.max(rows(c), axis=0, keepdims=True))
        else:
            m_new = jnp.maximum(m_prev, bmax_scr[slot * per + e:slot * per + e + 1, :])
        alpha = jnp.exp2(m_prev - m_new)
        pv = None
        for c in range(nslice):
            ks = pl.ds(pl.multiple_of(j * tq + c * rows_per, rows_per), rows_per)
            d = _dot(vcat_scr[e, :, ks], jnp.exp2(rows(c) - m_new).astype(BF))
            pv = d if pv is None else pv + d
        acc_scr[e] = alpha * acc_scr[e] + pv
        m_scr[e:e + 1, :] = m_new

    for e in range(per):
        scores(0, 0, e)

    def pair(t, carry):
        for e in range(per):
            scores(2 * t + 1, 1, e)
            attend(2 * t, 0, e, False)
        for e in range(per):
            scores(2 * t + 2, 0, e)
            attend(2 * t + 1, 1, e, False)
        return carry

    lax.fori_loop(0, i // 2, pair, 0)

    @pl.when(i % 2 == 0)
    def _():
        for e in range(per):
            attend(i, 0, e, True)

    @pl.when(i % 2 == 1)
    def _():
        for e in range(per):
            scores(i, 1, e)
            attend(i - 1, 0, e, False)
        for e in range(per):
            attend(i, 1, e, True)

    out_t = jnp.concatenate([acc_scr[e, :dh, :] / acc_scr[e, dh:dh + 1, :] for e in range(per)], axis=0)
    o_ref[0] = out_t.T


def fox_prompt(z, aug_q, k_tok, aug_k, vt_b, *, dh, width, tq=1024):
    n, l, _ = z.shape
    nblk = width // LANES
    tq = min(tq, l)
    body = functools.partial(_fox_body, tq=tq, dh=dh)
    qblk = pl.BlockSpec((1, tq, LANES), lambda b, j, i: (b, i, j))
    seq = pl.BlockSpec((1, l, LANES), lambda b, j, i: (b, 0, j))
    return pl.pallas_call(
        body,
        grid=(n, nblk, l // tq),
        in_specs=[qblk, qblk, seq, seq,
                  pl.BlockSpec((1, LANES, l), lambda b, j, i: (b, j, 0))],
        out_specs=qblk,
        out_shape=jax.ShapeDtypeStruct((n, l, width), F32),
        scratch_shapes=[pltpu.VMEM((l, 2 * LANES), BF), pltpu.VMEM((LANES // dh, dh + HEAD_PAD, l), BF),
                        pltpu.VMEM((LANES // dh, tq), F32), pltpu.VMEM((LANES // dh, dh + HEAD_PAD, tq), F32),
                        pltpu.VMEM((2 * (LANES // dh), tq, tq), F32), pltpu.VMEM((2 * (LANES // dh), tq), F32)],
        compiler_params=_params("parallel", "parallel", "arbitrary"),
        name="fox_prompt",
    )(z, aug_q, k_tok, aug_k, vt_b)


def _logf_pages_body(x_ref, o_ref):
    heads, pb, _ = x_ref.shape
    for h in range(HEAD_PAD):
        rows = x_ref[h] if h < heads else jnp.zeros(x_ref.shape[1:], F32)
        o_ref[pl.ds(h, pb, stride=HEAD_PAD), :] = rows


def logf_page_major(logf_t, *, pages_per_step=256):
    heads, n_phys, page = logf_t.shape
    pb = math.gcd(pages_per_step, n_phys)
    out = pl.pallas_call(
        _logf_pages_body,
        grid=(n_phys // pb,),
        in_specs=[pl.BlockSpec((heads, pb, page), lambda i: (0, i, 0))],
        out_specs=pl.BlockSpec((pb * HEAD_PAD, page), lambda i: (i, 0)),
        out_shape=jax.ShapeDtypeStruct((n_phys * HEAD_PAD, page), F32),
        compiler_params=_params("parallel"),
        name="logf_page_major",
    )(logf_t)
    return out.reshape(n_phys, HEAD_PAD, page)


def _decode_body(pt_ref, q_ref, kn_ref, vn_ref, bn_ref, *refs, g, heads, lq):
    k_pages = refs[:g]
    v_pages = refs[g:2 * g]
    f_pages = refs[2 * g:3 * g]
    o_ref = refs[3 * g]
    m_scr, l_scr, acc_scr, carry_scr, qbd_scr = refs[3 * g + 1:]
    j = pl.program_id(1)
    rows, width = qbd_scr.shape
    dh = width // heads
    own = (lax.broadcasted_iota(jnp.int32, (rows, width), 0) // lq
           == lax.broadcasted_iota(jnp.int32, (rows, width), 1) // dh)

    @pl.when(j == 0)
    def _():
        q = jnp.concatenate([q_ref[0] * (dh ** -0.5)] * heads, axis=0)
        qbd_scr[...] = jnp.where(own, q, 0.0).astype(BF)
        s = _dot(qbd_scr[...], kn_ref[0]) + bn_ref[0]
        m = jnp.max(s, axis=-1, keepdims=True)
        p = jnp.exp(s - m)
        m_scr[...] = m
        l_scr[...] = jnp.sum(p, axis=-1, keepdims=True)
        acc_scr[...] = _dot_nt(p.astype(BF), vn_ref[0])
        carry_scr[...] = jnp.zeros(carry_scr.shape, F32)

    qbd = qbd_scr[...]
    src = lax.broadcasted_iota(jnp.int32, (LANES, 2 * LANES), 0)
    dst = lax.broadcasted_iota(jnp.int32, (LANES, 2 * LANES), 1)
    w = jnp.where((dst >= LANES) | (src > dst), 1.0, 0.0).astype(BF)
    terms = jnp.concatenate([t for i in range(g) for t in _split3(f_pages[i][0])], axis=0)
    r_all = _dot(terms, w)
    carry = carry_scr[...]
    decay = [None] * g
    for i in reversed(range(g)):
        base = i * 3 * HEAD_PAD
        r = (r_all[base:base + HEAD_PAD] + r_all[base + HEAD_PAD:base + 2 * HEAD_PAD]
             + r_all[base + 2 * HEAD_PAD:base + 3 * HEAD_PAD])
        decay[i] = r[:, :LANES] + carry
        carry = carry + r[:, LANES:]
    carry_scr[...] = carry
    decay = jnp.concatenate(decay, axis=-1)

    s = jnp.concatenate([_dot(qbd, k_pages[i][0].astype(BF)) for i in range(g)], axis=-1)
    rows, cols = s.shape
    s = (s.reshape(heads, lq, cols) + decay[:heads, None, :]).reshape(rows, cols)
    m_prev = m_scr[...]
    m_new = jnp.maximum(m_prev, jnp.max(s, axis=-1, keepdims=True))
    alpha = jnp.exp(m_prev - m_new)
    p = jnp.exp(s - m_new).astype(BF)
    l_scr[...] = alpha * l_scr[...] + jnp.sum(p.astype(F32), axis=-1, keepdims=True)
    acc = alpha * acc_scr[...]
    page = cols // g
    for i in range(g):
        acc = acc + _dot_nt(p[:, i * page:(i + 1) * page], v_pages[i][0].astype(BF))
    acc_scr[...] = acc
    m_scr[...] = m_new

    @pl.when(j == pl.num_programs(1) - 1)
    def _():
        o = jnp.where(own, acc_scr[...] / l_scr[...], 0.0)
        out = o[0:lq]
        for h in range(1, heads):
            out = out + o[h * lq:(h + 1) * lq]
        o_ref[0] = out


def fox_decode(page_table, z, kn_t, vn_t, bias_new, k_pages_t, v_pages_t, logf_pages, *, heads, width):
    nb, npg = page_table.shape
    lq = z.shape[1]
    rows = heads * lq
    page = k_pages_t.shape[-1]
    g = min(DEC_PAGES_PER_STEP, npg)
    steps = npg // g

    def page_spec(i, sublanes):
        return pl.BlockSpec((1, sublanes, page), lambda b, j, pt: (pt[b, (steps - 1 - j) * g + i], 0, 0))

    fixed = lambda b, j, pt: (b, 0, 0)
    grid_spec = pltpu.PrefetchScalarGridSpec(
        num_scalar_prefetch=1,
        grid=(nb, steps),
        in_specs=[pl.BlockSpec((1, lq, width), fixed),
                  pl.BlockSpec((1, width, LANES), fixed),
                  pl.BlockSpec((1, width, LANES), fixed),
                  pl.BlockSpec((1, rows, LANES), fixed)]
                 + [page_spec(i, width) for i in range(g)] + [page_spec(i, width) for i in range(g)]
                 + [page_spec(i, HEAD_PAD) for i in range(g)],
        out_specs=pl.BlockSpec((1, lq, width), fixed),
        scratch_shapes=[pltpu.VMEM((rows, 1), F32), pltpu.VMEM((rows, 1), F32),
                        pltpu.VMEM((rows, width), F32), pltpu.VMEM((HEAD_PAD, LANES), F32),
                        pltpu.VMEM((rows, width), BF)],
    )
    return pl.pallas_call(
        functools.partial(_decode_body, g=g, heads=heads, lq=lq),
        grid_spec=grid_spec,
        out_shape=jax.ShapeDtypeStruct((nb, lq, width), F32),
        compiler_params=_params("parallel", "arbitrary"),
        name="fox_decode",
    )(page_table, z, kn_t, vn_t, bias_new, *([k_pages_t] * g), *([v_pages_t] * g), *([logf_pages] * g))


def kernel(x_prompt, x_sample, state_s5_re, state_s5_im, cache_mem_k, cache_mem_v, cache_k, cache_v, cache_logf, page_table, mem_prompt, norm_mix, norm_mlp, w_in, w_out, w_up, w_down, w_mem_kv, s5_a_re, s5_a_im, s5_log_step, s5_b_re, s5_b_im, s5_c_re, s5_c_im, s5_d, s5_w_glu, s5_b_glu, norm_kv, w_kv, w_f, b_f, norm_final):
    n_p, seq, d_model = x_prompt.shape
    n_s, dec_seq, _ = x_sample.shape
    depth = w_in.shape[0]
    n_a = s5_a_re.shape[0]
    assert depth == 2 and n_a == 1, "layer pattern: one S5 layer then one FoX layer"
    mem_heads, mem_dh = cache_mem_k.shape[3:]
    d_mem = mem_heads * mem_dh
    width = d_model - d_mem
    fox_heads, fox_dh = cache_k.shape[2:]
    page = cache_k.shape[1]
    group_ch = s5_b_re.shape[-1]
    s5_state = s5_a_re.shape[-1]
    per = LANES // group_ch
    mem_col = width // d_mem
    assert width % d_mem == 0 and width % LANES == 0 and LANES % fox_dh == 0 and LANES % group_ch == 0
    assert fox_heads <= HEAD_PAD and page == LANES
    assert dec_seq == S5_CHUNK and seq % S5_CHUNK == 0
    assert S5_CHUNK * group_ch == LANES and 2 * s5_state == LANES
    assert (seq // S5_CHUNK) % min(S5_ROWS_PER_TILE, seq // S5_CHUNK) == 0

    w_in_b = w_in.astype(BF)
    w_out_b = w_out.astype(BF)
    w_up_b = w_up.astype(BF)
    w_down_b = w_down.astype(BF)
    w_glu_b = s5_w_glu.astype(BF)
    g_mix = norm_mix.reshape(depth, 1, d_model)
    g_mlp = norm_mlp.reshape(depth, 1, d_model)
    w_memkv_t = w_mem_kv.transpose(0, 2, 1).astype(BF)
    wkv_t = w_kv.T.astype(BF)
    wk = w_kv[:, :width].astype(BF)
    wf_t = jnp.pad(w_f.T, ((0, HEAD_PAD - fox_heads), (0, 0))).astype(BF)
    bf_col = jnp.pad(b_f, (0, HEAD_PAD - fox_heads)).reshape(HEAD_PAD, 1)
    wf = jnp.pad(w_f, ((0, 0), (0, LANES - fox_heads))).astype(BF)
    bf_row = jnp.pad(b_f, (0, LANES - fox_heads)).reshape(1, LANES)
    place = fox_bias_placement(fox_heads, fox_dh, width)

    def mem_t(c):
        return c.transpose(0, 1, 3, 4, 2).reshape(c.shape[0], c.shape[1], d_mem, c.shape[2])

    k_pages_t = cache_k.transpose(0, 2, 3, 1).reshape(cache_k.shape[0], width, page)
    v_pages_t = cache_v.transpose(0, 2, 3, 1).reshape(cache_v.shape[0], width, page)
    logf_pages = logf_page_major(cache_logf.transpose(2, 0, 1))

    s5_ops = s5_operators(s5_a_re[0], s5_a_im[0], s5_log_step[0], s5_b_re[0], s5_b_im[0],
                          s5_c_re[0], s5_c_im[0], s5_d[0])

    def mem_attend(z, mem, layer):
        kt, vt, vblk = mem
        n, l, _ = z.shape
        return memory_attend(z, kt, vt, layer, heads=mem_heads, f=d_mem, col_block=mem_col,
                             v_row_block=vblk).reshape(n * l, d_mem)

    def layer0(x, z_view, h0, mem, per_row_state):
        n, l, _ = x.shape
        h = x.reshape(n * l, d_model)
        z = rms_matmul(h, g_mix, w_in_b, 0).reshape(n, l, d_model)
        y_mem = mem_attend(z, mem, 0)
        y_s5, hend = s5_core(z.reshape(z_view), s5_ops, h0, width=width, per_row_state=per_row_state)
        h = layer_tail(h, y_s5.reshape(n * l, width), y_mem, w_out_b, g_mlp, w_up_b, w_down_b, 0,
                       glu=(w_glu_b[0], s5_b_glu[0]))
        return h, hend

    def layer1_front(h, n, l, mem, nb, seq_len):
        z = rms_matmul(h, g_mix, w_in_b, 1).reshape(n, l, d_model)
        y_mem = mem_attend(z, mem, 1)
        kv = shared_kv_t(h, norm_kv, wkv_t, wk, wf_t, bf_col, wf, bf_row, place, nb=nb, seq_len=seq_len)
        return z, y_mem, kv

    def layer1_back(h, y_fox, y_mem):
        return layer_tail(h, y_fox, y_mem, w_out_b, g_mlp, w_up_b, w_down_b, 1, final_g=norm_final)

    memkv_t = memory_kv_t(mem_prompt, w_memkv_t)
    m_tok = mem_prompt.shape[1]
    pmk_t = memkv_t[:, :, :d_mem]
    pmv_t = memkv_t[:, :, d_mem:]
    to_mem = lambda a: a.reshape(depth, n_p, mem_heads, mem_dh, m_tok).transpose(0, 1, 4, 2, 3)
    p_mem_k = to_mem(pmk_t)
    p_mem_v = to_mem(pmv_t)

    sw = 2 * per * s5_state
    zeros = jnp.zeros((width // LANES, n_p, 1, sw), F32)
    p_mem = (memkv_t, memkv_t, 1)
    h, p_hend = layer0(x_prompt, (n_p, seq, d_model), zeros, p_mem, False)
    p_hr, p_hi = s5_state_out(p_hend[:, :, 0], per, s5_state)
    z, y_mem, (kt, vt, vtb, ktok, lft, _, aq, ak) = layer1_front(h, n_p, seq, p_mem, n_p, seq)
    tokmaj = lambda a: a.reshape(n_p, seq, width)
    y_fox = fox_prompt(z, tokmaj(aq), tokmaj(ktok), tokmaj(ak), vtb, dh=fox_dh, width=width)
    y_prompt = layer1_back(h, y_fox.reshape(n_p * seq, width), y_mem).reshape(n_p, seq, d_model)
    to_heads = lambda a, n, l: a.reshape(n, fox_heads, fox_dh, l).transpose(0, 3, 1, 2)
    p_k = to_heads(kt, n_p, seq)
    p_v = to_heads(vt, n_p, seq)
    p_logf = lft[:, :fox_heads].transpose(0, 2, 1)

    s_mem = (mem_t(cache_mem_k), mem_t(cache_mem_v), 0)
    tok = n_s * dec_seq
    s_h0 = s5_state_in(state_s5_re[0], state_s5_im[0], per)[:, None]
    hs, s_hend = layer0(x_sample, (1, tok, d_model), s_h0, s_mem, True)
    s_hr, s_hi = s5_state_out(s_hend[:, 0], per, s5_state)
    zs, ys_mem, (kts, vts, _, _, lfts, cums, _, _) = layer1_front(hs, n_s, dec_seq, s_mem, 1, dec_seq)
    new_heads = lambda a: a[0].reshape(fox_heads, fox_dh, n_s, dec_seq).transpose(2, 3, 0, 1)
    s_k = new_heads(kts)
    s_v = new_heads(vts)
    s_logf = lfts[0, :fox_heads].reshape(fox_heads, n_s, dec_seq).transpose(1, 2, 0)

    new_t = jnp.stack([kts[0], vts[0]]).reshape(2, width, n_s, dec_seq).transpose(0, 2, 1, 3)
    new_t = jnp.pad(new_t, ((0, 0), (0, 0), (0, 0), (0, LANES - dec_seq))).astype(BF)
    c_new = cums[:, :fox_heads].reshape(n_s, dec_seq, fox_heads).transpose(0, 2, 1)
    qi = jnp.arange(dec_seq)[:, None]
    tp = jnp.arange(LANES)[None, :]
    c_pad = jnp.pad(c_new, ((0, 0), (0, 0), (0, LANES - dec_seq)))
    bias_new = jnp.where((tp <= qi)[None, None], -c_pad[:, :, None, :], NEG_INF).reshape(
        n_s, fox_heads * dec_seq, LANES)
    ys_fox = fox_decode(page_table, zs, new_t[0], new_t[1], bias_new, k_pages_t, v_pages_t, logf_pages,
                        heads=fox_heads, width=width).reshape(tok, width)
    y_sample = layer1_back(hs, ys_fox, ys_mem).reshape(n_s, dec_seq, d_model)

    return (y_prompt, y_sample, p_hr[None], p_hi[None], p_mem_k, p_mem_v, p_k, p_v, p_logf,
            s_hr[None], s_hi[None], s_k, s_v, s_logf)
```

```python
import functools
import math

import jax
import jax.numpy as jnp
import numpy as np
from jax import lax
from jax.experimental import pallas as pl
from jax.experimental.pallas import tpu as pltpu

BF = jnp.bfloat16
F32 = jnp.float32
RMS_EPS = 1e-6
NEG_INF = -1e30
LOG2E = math.log2(math.e)
V7X_VMEM_BYTES = 64 * 1024 * 1024
VMEM_LIMIT = V7X_VMEM_BYTES - 8 * 1024 * 1024
LANES = 128
SUBLANES = 8
S5_CHUNK = SUBLANES
S5_ROWS_PER_TILE = 256
HEAD_PAD = 16
DEC_PAGES_PER_STEP = 16
MEM_SEQS_PER_STEP = 8
FOX_ROWS = 256

NT_DIMS = (((1,), (1,)), ((), ()))


def _params(*sem):
    return pltpu.CompilerParams(dimension_semantics=sem, vmem_limit_bytes=VMEM_LIMIT)


def _dot(a, b):
    return jnp.dot(a, b, preferred_element_type=F32)


def _dot_nt(a, b):
    return lax.dot_general(a, b, NT_DIMS, preferred_element_type=F32)


def _rms(x, g):
    ms = jnp.mean(x * x, axis=-1, keepdims=True)
    return x * lax.rsqrt(ms + RMS_EPS) * g


def _rms_matmul_body(x_ref, g_ref, w_ref, o_ref):
    xn = _rms(x_ref[...], g_ref[0])
    o_ref[...] = _dot(xn.astype(BF), w_ref[0])


def rms_matmul(x, g, w, layer, *, tm=1024):
    t, d = x.shape
    n = w.shape[2]
    tm = min(tm, t)
    return pl.pallas_call(
        _rms_matmul_body,
        grid=(t // tm,),
        in_specs=[pl.BlockSpec((tm, d), lambda i: (i, 0)),
                  pl.BlockSpec((1, 1, d), lambda i: (layer, 0, 0)),
                  pl.BlockSpec((1, d, n), lambda i: (layer, 0, 0))],
        out_specs=pl.BlockSpec((tm, n), lambda i: (i, 0)),
        out_shape=jax.ShapeDtypeStruct((t, n), F32),
        compiler_params=_params("parallel"),
        name="rms_matmul",
    )(x, g, w)


def _memkv_body(w_ref, m_ref, o_ref):
    o_ref[0, 0] = _dot_nt(w_ref[0], m_ref[0].astype(BF))


def memory_kv_t(mem, w_t):
    nb, m, d = mem.shape
    depth, f, _ = w_t.shape
    return pl.pallas_call(
        _memkv_body,
        grid=(depth, nb),
        in_specs=[pl.BlockSpec((1, f, d), lambda l, b: (l, 0, 0)),
                  pl.BlockSpec((1, m, d), lambda l, b: (b, 0, 0))],
        out_specs=pl.BlockSpec((1, 1, f, m), lambda l, b: (l, b, 0, 0)),
        out_shape=jax.ShapeDtypeStruct((depth, nb, f, m), F32),
        compiler_params=_params("parallel", "parallel"),
        name="memory_kv",
    )(w_t, mem)


def _memattn_body(q_ref, kt_ref, vt_ref, o_ref, *, heads, dh):
    for sq in range(q_ref.shape[0]):
        q = q_ref[sq] * (dh ** -0.5)
        kt = kt_ref[0, sq].astype(BF)
        vt = vt_ref[0, sq].astype(BF)
        lane = lax.broadcasted_iota(jnp.int32, q.shape, 1)
        out = jnp.zeros(q.shape, F32)
        for h in range(heads):
            sel = (lane >= h * dh) & (lane < (h + 1) * dh)
            s = _dot(jnp.where(sel, q, 0.0).astype(BF), kt)
            m = jnp.max(s, axis=-1, keepdims=True)
            p = jnp.exp(s - m)
            l = jnp.sum(p, axis=-1, keepdims=True)
            o = _dot_nt((p / l).astype(BF), vt)
            out = jnp.where(sel, o, out)
        o_ref[sq] = out


def memory_attend(z, kt, vt, layer, *, heads, f, col_block, k_row_block=0, v_row_block=0, tm=512):
    n, l, _ = z.shape
    m = kt.shape[-1]
    tm = min(tm, l)
    seqs = MEM_SEQS_PER_STEP if (l == tm and n % MEM_SEQS_PER_STEP == 0) else 1
    body = functools.partial(_memattn_body, heads=heads, dh=f // heads)
    return pl.pallas_call(
        body,
        grid=(n // seqs, l // tm),
        in_specs=[pl.BlockSpec((seqs, tm, f), lambda b, i: (b, i, col_block)),
                  pl.BlockSpec((1, seqs, f, m), lambda b, i: (layer, b, k_row_block, 0)),
                  pl.BlockSpec((1, seqs, f, m), lambda b, i: (layer, b, v_row_block, 0))],
        out_specs=pl.BlockSpec((seqs, tm, f), lambda b, i: (b, i, 0)),
        out_shape=jax.ShapeDtypeStruct((n, l, f), F32),
        compiler_params=_params("parallel", "parallel"),
        name="memory_attend",
    )(z, kt, vt)


def _cmul(a, h):
    return a[0:1, :] * h + a[1:2, :] * pltpu.roll(h, h.shape[-1] // 2, axis=1)


def _s5_expand(src_ref, dst_ref, *, row_inner, col_inner):
    per = src_ref.shape[1] // LANES
    wide = per * LANES
    r = lax.broadcasted_iota(jnp.int32, (LANES, wide), 0)
    q = lax.broadcasted_iota(jnp.int32, (LANES, wide), 1)
    tile = jnp.where((r // col_inner == q // (per * col_inner)) & (r % col_inner == q % col_inner),
                     1.0, 0.0).astype(BF)
    spread = _dot(src_ref[0], tile).astype(BF)
    qg = (lax.broadcasted_iota(jnp.int32, (row_inner, wide), 1) // col_inner) % per
    for g in range(per):
        for a in range(LANES // row_inner):
            blk = spread[g * LANES + a * row_inner:g * LANES + (a + 1) * row_inner, :]
            dst_ref[pl.ds((a * per + g) * row_inner, row_inner), :] = jnp.where(qg == g, blk, jnp.zeros_like(blk))


def _s5_body(z_ref, kc_ref, wc_ref, oc_ref, apow_ref, h0_ref, y_ref, hend_ref, carry_ref,
             kx_scr, wst_scr, wout_scr, *, rt, per_row_state, group_ch, state):
    i = pl.program_id(2)
    t_len = S5_CHUNK

    @pl.when((pl.program_id(1) == 0) & (i == 0))
    def _():
        _s5_expand(kc_ref, kx_scr, row_inner=group_ch, col_inner=group_ch)
        _s5_expand(wc_ref, wst_scr, row_inner=group_ch, col_inner=state)
        _s5_expand(oc_ref, wout_scr, row_inner=state, col_inner=group_ch)

    lhs = jnp.concatenate([z_ref[0, pl.ds(t, rt, stride=t_len), :].astype(BF) for t in range(t_len)],
                          axis=-1)
    y = _dot(lhs, kx_scr[...])
    hs = _dot(lhs, wst_scr[...])
    a_chunk = apow_ref[0, 0]
    if per_row_state:
        h_in = h0_ref[0, 0]
        hs = hs + _cmul(a_chunk, h_in)
        hend_ref[0, 0] = hs
    else:
        @pl.when(i == 0)
        def _():
            carry_ref[...] = h0_ref[0, 0]
        h0 = carry_ref[...]
        rid = lax.broadcasted_iota(jnp.int32, hs.shape, 0)
        hs = hs + jnp.where(rid == 0, _cmul(a_chunk, h0), 0.0)
        step, k = 1, 0
        while step < rt:
            shifted = jnp.where(rid >= step, pltpu.roll(hs, step, axis=0), 0.0)
            hs = hs + _cmul(apow_ref[0, k], shifted)
            step *= 2
            k += 1
        h_in = jnp.where(rid == 0, h0, pltpu.roll(hs, 1, axis=0))
        carry_ref[...] = hs[rt - 1:rt, :]
        hend_ref[0, 0] = hs[rt - 1:rt, :]
    y = y + _dot(h_in.astype(BF), wout_scr[...])
    for t in range(t_len):
        y_ref[0, pl.ds(t, rt, stride=t_len), :] = y[:, t * LANES:(t + 1) * LANES]


def s5_core(z, ops, h0, *, width, per_row_state):
    kc, wc, oc, apow = ops
    nb, l, _ = z.shape
    nblk = width // LANES
    sw = kc.shape[1]
    state = sw // 2 // (sw // LANES)
    rows_total = l // S5_CHUNK
    rt = rows_total if per_row_state else min(S5_ROWS_PER_TILE, rows_total)
    tiles = rows_total // rt
    rs = h0.shape[2]
    nk = apow.shape[1]
    body = functools.partial(_s5_body, rt=rt, per_row_state=per_row_state,
                             group_ch=LANES // (sw // LANES), state=state)
    compact = pl.BlockSpec((1, sw, LANES), lambda j, b, i: (j, 0, 0))
    return pl.pallas_call(
        body,
        grid=(nblk, nb, tiles),
        in_specs=[pl.BlockSpec((1, rt * S5_CHUNK, LANES), lambda j, b, i: (b, i, j)),
                  compact, compact, compact,
                  pl.BlockSpec((1, nk, 2, sw), lambda j, b, i: (j, 0, 0, 0)),
                  pl.BlockSpec((1, 1, rs, sw), lambda j, b, i: (j, b, 0, 0))],
        out_specs=[pl.BlockSpec((1, rt * S5_CHUNK, LANES), lambda j, b, i: (b, i, j)),
                   pl.BlockSpec((1, 1, rs, sw), lambda j, b, i: (j, b, 0, 0))],
        out_shape=[jax.ShapeDtypeStruct((nb, l, width), F32),
                   jax.ShapeDtypeStruct((nblk, nb, rs, sw), F32)],
        scratch_shapes=[pltpu.VMEM((1, sw), F32), pltpu.VMEM((sw, sw), BF), pltpu.VMEM((sw, sw), BF),
                        pltpu.VMEM((sw, sw), BF)],
        compiler_params=_params("arbitrary", "arbitrary", "arbitrary"),
        name="s5_core",
    )(z, kc, wc, oc, apow, h0)


def s5_operators(a_re, a_im, log_step, b_re, b_im, c_re, c_im, d):
    chunk = S5_CHUNK
    g, p = a_re.shape
    c = b_re.shape[-1]
    per = LANES // c
    nblk = g // per
    dt = jnp.exp(log_step.astype(F32))[:, None]
    a_re = a_re.astype(F32)
    a_im = a_im.astype(F32)
    lam_re = a_re * dt
    lam_im = a_im * dt
    mag = jnp.exp(lam_re)
    ab_re = mag * jnp.cos(lam_im)
    ab_im = mag * jnp.sin(lam_im)
    den = a_re * a_re + a_im * a_im
    nr = ab_re - 1.0
    ni = ab_im
    cr = (nr * a_re + ni * a_im) / den
    ci = (ni * a_re - nr * a_im) / den
    bb_re = cr[..., None] * b_re - ci[..., None] * b_im
    bb_im = cr[..., None] * b_im + ci[..., None] * b_re

    def power(tau):
        tau = tau.astype(F32)[None, :, None]
        m = jnp.exp(lam_re[:, None, :] * tau)
        ph = lam_im[:, None, :] * tau
        return m * jnp.cos(ph), m * jnp.sin(ph)

    pw_re, pw_im = power(jnp.arange(chunk + 1))
    e_re = pw_re[..., None] * bb_re[:, None] - pw_im[..., None] * bb_im[:, None]
    e_im = pw_re[..., None] * bb_im[:, None] + pw_im[..., None] * bb_re[:, None]
    et_re = e_re[:, :chunk].transpose(0, 1, 3, 2)[:, :, None]
    et_im = e_im[:, :chunk].transpose(0, 1, 3, 2)[:, :, None]
    ktau = jnp.sum(c_re[:, None, :, None, :] * et_re - c_im[:, None, :, None, :] * et_im, axis=-1)
    ktau = ktau.at[:, 0].add(d[:, :, None] * jnp.eye(c, dtype=F32)[None])
    s_idx = jnp.arange(chunk)[:, None]
    t_idx = jnp.arange(chunk)[None, :]
    lag = t_idx - s_idx
    toe = jnp.where((lag >= 0)[None, :, :, None, None], ktau[:, jnp.maximum(lag, 0)], 0.0)
    kc = toe.transpose(0, 1, 4, 2, 3).reshape(nblk, per * chunk * c, chunk * c)

    rev = chunk - 1 - jnp.arange(chunk)
    w_c = jnp.stack([e_re[:, rev], e_im[:, rev]], axis=2)
    wc = w_c.transpose(0, 1, 4, 2, 3).reshape(nblk, per * chunk * c, 2 * p)

    q_re = pw_re[:, 1:chunk + 1]
    q_im = pw_im[:, 1:chunk + 1]
    wo_re = c_re[:, None] * q_re[:, :, None, :] - c_im[:, None] * q_im[:, :, None, :]
    wo_im = -(c_re[:, None] * q_im[:, :, None, :] + c_im[:, None] * q_re[:, :, None, :])
    o_c = jnp.stack([wo_re, wo_im], axis=1)
    oc = o_c.transpose(0, 1, 4, 2, 3).reshape(nblk, per * 2 * p, chunk * c)

    nk = max(1, int(math.log2(S5_ROWS_PER_TILE)))
    sc_re, sc_im = power(chunk * (2 ** jnp.arange(nk)))
    lay = lambda x: x.reshape(nblk, per, nk, p).transpose(0, 2, 1, 3).reshape(nblk, nk, per * p)
    sc_re, sc_im = lay(sc_re), lay(sc_im)
    apow = jnp.stack([jnp.concatenate([sc_re, sc_re], -1), jnp.concatenate([-sc_im, sc_im], -1)], axis=2)
    return kc.astype(BF), wc.astype(BF), oc.astype(BF), apow


def s5_state_in(h_re, h_im, per):
    n, g, p = h_re.shape
    f = lambda x: x.reshape(n, g // per, per * p)
    return jnp.concatenate([f(h_re), f(h_im)], axis=-1).transpose(1, 0, 2)


def s5_state_out(h, per, p):
    nblk, n, w2 = h.shape
    h = h.transpose(1, 0, 2)
    f = lambda x: x.reshape(n, nblk * per, p)
    return f(h[..., :w2 // 2]), f(h[..., w2 // 2:])


def _gelu_tanh(x):
    return 0.5 * x * (1.0 + jnp.tanh(math.sqrt(2.0 / math.pi) * (x + 0.044715 * (x * x * x))))


def _tail_body(*refs, glu, final_norm, width):
    it = iter(refs)
    h_ref, ym_ref, ymem_ref, wo_ref = next(it), next(it), next(it), next(it)
    if glu:
        wg_ref, bg_ref = next(it), next(it)
    gm_ref, wu_ref, wd_ref = next(it), next(it), next(it)
    if final_norm:
        gf_ref = next(it)
    o_ref, xn_scr = next(it), next(it)
    c = pl.program_id(1)

    @pl.when(c == 0)
    def _():
        ym = ym_ref[...]
        if glu:
            zz = _gelu_tanh(ym)
            gate = jax.nn.sigmoid(_dot(zz.astype(BF), wg_ref[...]) + bg_ref[...])
            ym = zz * gate
        d = wo_ref.shape[1]
        h = (h_ref[...] + _dot(ym.astype(BF), wo_ref[0, 0:width, :])
             + _dot(ymem_ref[...].astype(BF), wo_ref[0, width:d, :]))
        xn_scr[...] = _rms(h, gm_ref[0]).astype(BF)
        o_ref[...] = h

    up = jnp.maximum(_dot(xn_scr[...], wu_ref[0]), 0.0)
    o_ref[...] += _dot((up * up).astype(BF), wd_ref[0])

    if final_norm:
        @pl.when(c == pl.num_programs(1) - 1)
        def _():
            o_ref[...] = _rms(o_ref[...], gf_ref[...])


def layer_tail(h, y_main, y_mem, w_out, g_mlp, w_up, w_down, layer, *, glu=None, final_g=None, tm=1024,
               ff_chunk=2048):
    t, d = h.shape
    width = y_main.shape[1]
    wm = y_mem.shape[1]
    dff = w_up.shape[2]
    tm = min(tm, t)
    ff_chunk = min(ff_chunk, dff)
    row = lambda i, c: (i, 0)
    fix = lambda i, c: (0, 0)
    args = [h, y_main, y_mem, w_out]
    specs = [pl.BlockSpec((tm, d), row), pl.BlockSpec((tm, width), row), pl.BlockSpec((tm, wm), row),
             pl.BlockSpec((1, d, d), lambda i, c: (layer, 0, 0))]
    if glu is not None:
        args += [glu[0], glu[1].reshape(1, width)]
        specs += [pl.BlockSpec((width, width), fix), pl.BlockSpec((1, width), fix)]
    args += [g_mlp, w_up, w_down]
    specs += [pl.BlockSpec((1, 1, d), lambda i, c: (layer, 0, 0)),
              pl.BlockSpec((1, d, ff_chunk), lambda i, c: (layer, 0, c)),
              pl.BlockSpec((1, ff_chunk, d), lambda i, c: (layer, c, 0))]
    if final_g is not None:
        args.append(final_g.reshape(1, d))
        specs.append(pl.BlockSpec((1, d), fix))
    body = functools.partial(_tail_body, glu=glu is not None, final_norm=final_g is not None, width=width)
    return pl.pallas_call(
        body,
        grid=(t // tm, dff // ff_chunk),
        in_specs=specs,
        out_specs=pl.BlockSpec((tm, d), row),
        out_shape=jax.ShapeDtypeStruct((t, d), F32),
        scratch_shapes=[pltpu.VMEM((tm, d), BF)],
        compiler_params=_params("parallel", "arbitrary"),
        name="layer_tail",
    )(*args)


def _log_sigmoid(x):
    return jnp.minimum(x, 0.0) - jnp.log(1.0 + jnp.exp(-jnp.abs(x)))


def _chop(x):
    return lax.bitcast_convert_type(lax.bitcast_convert_type(x, jnp.uint32) & jnp.uint32(0xFFFF0000), F32)


def _split3(x):
    hi = _chop(x)
    mid = _chop(x - hi)
    lo = x - hi - mid
    return hi.astype(BF), mid.astype(BF), lo.astype(BF)


def _kvf_body(x_ref, g_ref, wkvt_ref, wk_ref, wft_ref, bfc_ref, wf_ref, bfr_ref, pq_ref, cq_ref, pk_ref, ck_ref,
              kt_ref, vt_ref, vtb_ref, ktok_ref, lft_ref, cum_ref, aq_ref, ak_ref, carry_ref,
              *, seq_len, tm, width):
    i = pl.program_id(1)
    hn = _rms(x_ref[...], g_ref[...]).astype(BF)
    kv = _dot_nt(wkvt_ref[...], hn)
    kt_ref[0] = kv[:width]
    vt_ref[0] = kv[width:]
    vtb_ref[0] = kv[width:].astype(BF)
    ktok_ref[...] = _dot(hn, wk_ref[...]).astype(BF)
    lft_ref[0] = _log_sigmoid(_dot_nt(wft_ref[...], hn) + bfc_ref[...])
    lf = _log_sigmoid(_dot(hn, wf_ref[...]) + bfr_ref[...])
    dst = lax.broadcasted_iota(jnp.int32, (tm, tm), 0)
    src = lax.broadcasted_iota(jnp.int32, (tm, tm), 1)
    keep = src <= dst
    if seq_len < tm:
        keep = keep & ((src // seq_len) == (dst // seq_len))
    tri = jnp.where(keep, 1.0, 0.0).astype(BF)
    hi, mid, lo = _split3(lf)
    cum = _dot(tri, hi) + _dot(tri, mid) + _dot(tri, lo)
    if seq_len > tm:
        @pl.when(i % (seq_len // tm) == 0)
        def _():
            carry_ref[...] = jnp.zeros(carry_ref.shape, F32)
        cum = cum + carry_ref[...]
        carry_ref[...] = cum[tm - 1:tm, :]
    cum_ref[...] = cum
    parts = jnp.concatenate(_split3(cum * LOG2E), axis=-1)
    aq_ref[...] = (_dot(parts, pq_ref[...]) + cq_ref[...]).astype(BF)
    ak_ref[...] = (_dot(parts, pk_ref[...]) + ck_ref[...]).astype(BF)


def fox_bias_placement(heads, dh, width):
    per = LANES // dh
    pq = np.zeros((3 * LANES, width), np.float32)
    pk = np.zeros((3 * LANES, width), np.float32)
    cq = np.zeros((1, width), np.float32)
    ck = np.zeros((1, width), np.float32)
    for h in range(heads):
        base = (h // per) * LANES + (h % per) * SUBLANES
        for term in range(3):
            pq[term * LANES + h, base + term] = 1.0
            pk[term * LANES + h, base + 3 + term] = -1.0
        cq[0, base + 3:base + 6] = 1.0
        ck[0, base:base + 3] = 1.0
    return jnp.asarray(pq, BF), jnp.asarray(cq), jnp.asarray(pk, BF), jnp.asarray(ck)


def shared_kv_t(h, g, wkv_t, wk, wf_t, bf_col, wf, bf_row, place, *, nb, seq_len, tm=512):
    t, d = h.shape
    lb = t // nb
    tm = min(tm, lb)
    f2 = wkv_t.shape[0]
    width = f2 // 2
    tiles = lb // tm
    body = functools.partial(_kvf_body, seq_len=seq_len, tm=tm, width=width)
    fix = lambda b, i: (0, 0)
    tok = lambda b, i: (b * tiles + i, 0)
    feat = lambda b, i: (b, 0, i)
    return pl.pallas_call(
        body,
        grid=(nb, tiles),
        in_specs=[pl.BlockSpec((tm, d), tok),
                  pl.BlockSpec((1, d), fix),
                  pl.BlockSpec((f2, d), fix),
                  pl.BlockSpec((d, width), fix),
                  pl.BlockSpec((HEAD_PAD, d), fix),
                  pl.BlockSpec((HEAD_PAD, 1), fix),
                  pl.BlockSpec((d, LANES), fix),
                  pl.BlockSpec((1, LANES), fix),
                  pl.BlockSpec((3 * LANES, width), fix),
                  pl.BlockSpec((1, width), fix),
                  pl.BlockSpec((3 * LANES, width), fix),
                  pl.BlockSpec((1, width), fix)],
        out_specs=[pl.BlockSpec((1, width, tm), feat),
                   pl.BlockSpec((1, width, tm), feat),
                   pl.BlockSpec((1, width, tm), feat),
                   pl.BlockSpec((tm, width), tok),
                   pl.BlockSpec((1, HEAD_PAD, tm), feat),
                   pl.BlockSpec((tm, LANES), tok),
                   pl.BlockSpec((tm, width), tok),
                   pl.BlockSpec((tm, width), tok)],
        out_shape=[jax.ShapeDtypeStruct((nb, width, lb), F32),
                   jax.ShapeDtypeStruct((nb, width, lb), F32),
                   jax.ShapeDtypeStruct((nb, width, lb), BF),
                   jax.ShapeDtypeStruct((t, width), BF),
                   jax.ShapeDtypeStruct((nb, HEAD_PAD, lb), F32),
                   jax.ShapeDtypeStruct((t, LANES), F32),
                   jax.ShapeDtypeStruct((t, width), BF),
                   jax.ShapeDtypeStruct((t, width), BF)],
        scratch_shapes=[pltpu.VMEM((1, LANES), F32)],
        compiler_params=_params("parallel", "arbitrary"),
        name="shared_kv",
    )(h, g.reshape(1, d), wkv_t, wk, wf_t, bf_col, wf, bf_row, *place)


def _fox_body(q_ref, aq_ref, k_ref, ak_ref, vt_ref, o_ref, kcat_scr, vcat_scr, m_scr, acc_scr, st_scr, bmax_scr,
              *, tq, dh):
    i = pl.program_id(2)
    per = LANES // dh

    @pl.when(i == 0)
    def _():
        kcat_scr[:, :LANES] = k_ref[0]
        kcat_scr[:, LANES:] = ak_ref[0]
        for e in range(per):
            vcat_scr[e, :dh, :] = vt_ref[0, e * dh:(e + 1) * dh, :]
            vcat_scr[e, dh:, :] = jnp.ones((vcat_scr.shape[1] - dh, vcat_scr.shape[2]), BF)

    qcat_t = jnp.concatenate([(q_ref[0] * (dh ** -0.5 * LOG2E)).T.astype(BF),
                              aq_ref[0].astype(F32).T.astype(BF)], axis=0)
    rows_per = min(FOX_ROWS, tq)
    nslice = tq // rows_per
    frow = lax.broadcasted_iota(jnp.int32, qcat_t.shape, 0)
    key_i = lax.broadcasted_iota(jnp.int32, (rows_per, tq), 0)
    qry_i = lax.broadcasted_iota(jnp.int32, (rows_per, tq), 1)
    qes = []
    for e in range(per):
        sel = ((frow >= e * dh) & (frow < (e + 1) * dh)) | (
            (frow >= LANES + e * SUBLANES) & (frow < LANES + (e + 1) * SUBLANES))
        qes.append(jnp.where(sel, qcat_t, jnp.zeros_like(qcat_t)))
    m_scr[...] = jnp.full(m_scr.shape, NEG_INF, F32)
    acc_scr[...] = jnp.zeros(acc_scr.shape, F32)

    def block_rows(j):
        return pl.ds(pl.multiple_of(j * tq, tq), tq)

    def scores(j, slot, e):
        st = _dot(kcat_scr[block_rows(j), :], qes[e])
        st_scr[slot * per + e] = st
        bmax_scr[slot * per + e:slot * per + e + 1, :] = jnp.max(st, axis=0, keepdims=True)

    def attend(j, slot, e, masked):
        def rows(c):
            st = st_scr[slot * per + e, c * rows_per:(c + 1) * rows_per, :]
            if masked:
                st = jnp.where(key_i + c * rows_per <= qry_i, st, NEG_INF)
            return st
        m_prev = m_scr[e:e + 1, :]
        if masked:
            m_new = m_prev
            for c in range(nslice):
                m_new = jnp.maximum(m_new, jnp.max(rows(c), axis=0, keepdims=True))
        else:
            m_new = jnp.maximum(m_prev, bmax_scr[slot * per + e:slot * per + e + 1, :])
        alpha = jnp.exp2(m_prev - m_new)
        pv = None
        for c in range(nslice):
            ks = pl.ds(pl.multiple_of(j * tq + c * rows_per, rows_per), rows_per)
            d = _dot(vcat_scr[e, :, ks], jnp.exp2(rows(c) - m_new).astype(BF))
            pv = d if pv is None else pv + d
        acc_scr[e] = alpha * acc_scr[e] + pv
        m_scr[e:e + 1, :] = m_new

    for e in range(per):
        scores(0, 0, e)

    def pair(t, carry):
        for e in range(per):
            scores(2 * t + 1, 1, e)
            attend(2 * t, 0, e, False)
        for e in range(per):
            scores(2 * t + 2, 0, e)
            attend(2 * t + 1, 1, e, False)
        return carry

    lax.fori_loop(0, i // 2, pair, 0)

    @pl.when(i % 2 == 0)
    def _():
        for e in range(per):
            attend(i, 0, e, True)

    @pl.when(i % 2 == 1)
    def _():
        for e in range(per):
            scores(i, 1, e)
            attend(i - 1, 0, e, False)
        for e in range(per):
            attend(i, 1, e, True)

    out_t = jnp.concatenate([acc_scr[e, :dh, :] / acc_scr[e, dh:dh + 1, :] for e in range(per)], axis=0)
    o_ref[0] = out_t.T


def fox_prompt(z, aug_q, k_tok, aug_k, vt_b, *, dh, width, tq=1024):
    n, l, _ = z.shape
    nblk = width // LANES
    tq = min(tq, l)
    body = functools.partial(_fox_body, tq=tq, dh=dh)
    qblk = pl.BlockSpec((1, tq, LANES), lambda b, j, i: (b, i, j))
    seq = pl.BlockSpec((1, l, LANES), lambda b, j, i: (b, 0, j))
    return pl.pallas_call(
        body,
        grid=(n, nblk, l // tq),
        in_specs=[qblk, qblk, seq, seq,
                  pl.BlockSpec((1, LANES, l), lambda b, j, i: (b, j, 0))],
        out_specs=qblk,
        out_shape=jax.ShapeDtypeStruct((n, l, width), F32),
        scratch_shapes=[pltpu.VMEM((l, 2 * LANES), BF), pltpu.VMEM((LANES // dh, dh + HEAD_PAD, l), BF),
                        pltpu.VMEM((LANES // dh, tq), F32), pltpu.VMEM((LANES // dh, dh + HEAD_PAD, tq), F32),
                        pltpu.VMEM((2 * (LANES // dh), tq, tq), F32), pltpu.VMEM((2 * (LANES // dh), tq), F32)],
        compiler_params=_params("parallel", "parallel", "arbitrary"),
        name="fox_prompt",
    )(z, aug_q, k_tok, aug_k, vt_b)


def _logf_pages_body(x_ref, o_ref):
    heads, pb, _ = x_ref.shape
    for h in range(HEAD_PAD):
        rows = x_ref[h] if h < heads else jnp.zeros(x_ref.shape[1:], F32)
        o_ref[pl.ds(h, pb, stride=HEAD_PAD), :] = rows


def logf_page_major(logf_t, *, pages_per_step=256):
    heads, n_phys, page = logf_t.shape
    pb = math.gcd(pages_per_step, n_phys)
    out = pl.pallas_call(
        _logf_pages_body,
        grid=(n_phys // pb,),
        in_specs=[pl.BlockSpec((heads, pb, page), lambda i: (0, i, 0))],
        out_specs=pl.BlockSpec((pb * HEAD_PAD, page), lambda i: (i, 0)),
        out_shape=jax.ShapeDtypeStruct((n_phys * HEAD_PAD, page), F32),
        compiler_params=_params("parallel"),
        name="logf_page_major",
    )(logf_t)
    return out.reshape(n_phys, HEAD_PAD, page)


def _decode_body(pt_ref, q_ref, kn_ref, vn_ref, bn_ref, *refs, g, heads, lq):
    k_pages = refs[:g]
    v_pages = refs[g:2 * g]
    f_pages = refs[2 * g:3 * g]
    o_ref = refs[3 * g]
    m_scr, l_scr, acc_scr, carry_scr, qbd_scr = refs[3 * g + 1:]
    j = pl.program_id(1)
    rows, width = qbd_scr.shape
    dh = width // heads
    own = (lax.broadcasted_iota(jnp.int32, (rows, width), 0) // lq
           == lax.broadcasted_iota(jnp.int32, (rows, width), 1) // dh)

    @pl.when(j == 0)
    def _():
        q = jnp.concatenate([q_ref[0] * (dh ** -0.5)] * heads, axis=0)
        qbd_scr[...] = jnp.where(own, q, 0.0).astype(BF)
        s = _dot(qbd_scr[...], kn_ref[0]) + bn_ref[0]
        m = jnp.max(s, axis=-1, keepdims=True)
        p = jnp.exp(s - m)
        m_scr[...] = m
        l_scr[...] = jnp.sum(p, axis=-1, keepdims=True)
        acc_scr[...] = _dot_nt(p.astype(BF), vn_ref[0])
        carry_scr[...] = jnp.zeros(carry_scr.shape, F32)

    qbd = qbd_scr[...]
    src = lax.broadcasted_iota(jnp.int32, (LANES, 2 * LANES), 0)
    dst = lax.broadcasted_iota(jnp.int32, (LANES, 2 * LANES), 1)
    w = jnp.where((dst >= LANES) | (src > dst), 1.0, 0.0).astype(BF)
    terms = jnp.concatenate([t for i in range(g) for t in _split3(f_pages[i][0])], axis=0)
    r_all = _dot(terms, w)
    carry = carry_scr[...]
    decay = [None] * g
    for i in reversed(range(g)):
        base = i * 3 * HEAD_PAD
        r = (r_all[base:base + HEAD_PAD] + r_all[base + HEAD_PAD:base + 2 * HEAD_PAD]
             + r_all[base + 2 * HEAD_PAD:base + 3 * HEAD_PAD])
        decay[i] = r[:, :LANES] + carry
        carry = carry + r[:, LANES:]
    carry_scr[...] = carry
    decay = jnp.concatenate(decay, axis=-1)

    s = _dot(qbd, jnp.concatenate([k_pages[i][0].astype(BF) for i in range(g)], axis=-1))
    rows, cols = s.shape
    s = (s.reshape(heads, lq, cols) + decay[:heads, None, :]).reshape(rows, cols)
    m_prev = m_scr[...]
    m_new = jnp.maximum(m_prev, jnp.max(s, axis=-1, keepdims=True))
    alpha = jnp.exp(m_prev - m_new)
    p = jnp.exp(s - m_new).astype(BF)
    l_scr[...] = alpha * l_scr[...] + jnp.sum(p.astype(F32), axis=-1, keepdims=True)
    acc = alpha * acc_scr[...]
    page = cols // g
    acc = acc + _dot_nt(p, jnp.concatenate([v_pages[i][0].astype(BF) for i in range(g)], axis=-1))
    acc_scr[...] = acc
    m_scr[...] = m_new

    @pl.when(j == pl.num_programs(1) - 1)
    def _():
        o = jnp.where(own, acc_scr[...] / l_scr[...], 0.0)
        out = o[0:lq]
        for h in range(1, heads):
            out = out + o[h * lq:(h + 1) * lq]
        o_ref[0] = out


def fox_decode(page_table, z, kn_t, vn_t, bias_new, k_pages_t, v_pages_t, logf_pages, *, heads, width):
    nb, npg = page_table.shape
    lq = z.shape[1]
    rows = heads * lq
    page = k_pages_t.shape[-1]
    g = min(DEC_PAGES_PER_STEP, npg)
    steps = npg // g

    def page_spec(i, sublanes):
        return pl.BlockSpec((1, sublanes, page), lambda b, j, pt: (pt[b, (steps - 1 - j) * g + i], 0, 0))

    fixed = lambda b, j, pt: (b, 0, 0)
    grid_spec = pltpu.PrefetchScalarGridSpec(
        num_scalar_prefetch=1,
        grid=(nb, steps),
        in_specs=[pl.BlockSpec((1, lq, width), fixed),
                  pl.BlockSpec((1, width, LANES), fixed),
                  pl.BlockSpec((1, width, LANES), fixed),
                  pl.BlockSpec((1, rows, LANES), fixed)]
                 + [page_spec(i, width) for i in range(g)] + [page_spec(i, width) for i in range(g)]
                 + [page_spec(i, HEAD_PAD) for i in range(g)],
        out_specs=pl.BlockSpec((1, lq, width), fixed),
        scratch_shapes=[pltpu.VMEM((rows, 1), F32), pltpu.VMEM((rows, 1), F32),
                        pltpu.VMEM((rows, width), F32), pltpu.VMEM((HEAD_PAD, LANES), F32),
                        pltpu.VMEM((rows, width), BF)],
    )
    return pl.pallas_call(
        functools.partial(_decode_body, g=g, heads=heads, lq=lq),
        grid_spec=grid_spec,
        out_shape=jax.ShapeDtypeStruct((nb, lq, width), F32),
        compiler_params=_params("parallel", "arbitrary"),
        name="fox_decode",
    )(page_table, z, kn_t, vn_t, bias_new, *([k_pages_t] * g), *([v_pages_t] * g), *([logf_pages] * g))


def kernel(x_prompt, x_sample, state_s5_re, state_s5_im, cache_mem_k, cache_mem_v, cache_k, cache_v, cache_logf, page_table, mem_prompt, norm_mix, norm_mlp, w_in, w_out, w_up, w_down, w_mem_kv, s5_a_re, s5_a_im, s5_log_step, s5_b_re, s5_b_im, s5_c_re, s5_c_im, s5_d, s5_w_glu, s5_b_glu, norm_kv, w_kv, w_f, b_f, norm_final):
    n_p, seq, d_model = x_prompt.shape
    n_s, dec_seq, _ = x_sample.shape
    depth = w_in.shape[0]
    n_a = s5_a_re.shape[0]
    assert depth == 2 and n_a == 1, "layer pattern: one S5 layer then one FoX layer"
    mem_heads, mem_dh = cache_mem_k.shape[3:]
    d_mem = mem_heads * mem_dh
    width = d_model - d_mem
    fox_heads, fox_dh = cache_k.shape[2:]
    page = cache_k.shape[1]
    group_ch = s5_b_re.shape[-1]
    s5_state = s5_a_re.shape[-1]
    per = LANES // group_ch
    mem_col = width // d_mem
    assert width % d_mem == 0 and width % LANES == 0 and LANES % fox_dh == 0 and LANES % group_ch == 0
    assert fox_heads <= HEAD_PAD and page == LANES
    assert dec_seq == S5_CHUNK and seq % S5_CHUNK == 0
    assert S5_CHUNK * group_ch == LANES and 2 * s5_state == LANES
    assert (seq // S5_CHUNK) % min(S5_ROWS_PER_TILE, seq // S5_CHUNK) == 0

    w_in_b = w_in.astype(BF)
    w_out_b = w_out.astype(BF)
    w_up_b = w_up.astype(BF)
    w_down_b = w_down.astype(BF)
    w_glu_b = s5_w_glu.astype(BF)
    g_mix = norm_mix.reshape(depth, 1, d_model)
    g_mlp = norm_mlp.reshape(depth, 1, d_model)
    w_memkv_t = w_mem_kv.transpose(0, 2, 1).astype(BF)
    wkv_t = w_kv.T.astype(BF)
    wk = w_kv[:, :width].astype(BF)
    wf_t = jnp.pad(w_f.T, ((0, HEAD_PAD - fox_heads), (0, 0))).astype(BF)
    bf_col = jnp.pad(b_f, (0, HEAD_PAD - fox_heads)).reshape(HEAD_PAD, 1)
    wf = jnp.pad(w_f, ((0, 0), (0, LANES - fox_heads))).astype(BF)
    bf_row = jnp.pad(b_f, (0, LANES - fox_heads)).reshape(1, LANES)
    place = fox_bias_placement(fox_heads, fox_dh, width)

    def mem_t(c):
        return c.transpose(0, 1, 3, 4, 2).reshape(c.shape[0], c.shape[1], d_mem, c.shape[2])

    k_pages_t = cache_k.transpose(0, 2, 3, 1).reshape(cache_k.shape[0], width, page)
    v_pages_t = cache_v.transpose(0, 2, 3, 1).reshape(cache_v.shape[0], width, page)
    logf_pages = logf_page_major(cache_logf.transpose(2, 0, 1))

    s5_ops = s5_operators(s5_a_re[0], s5_a_im[0], s5_log_step[0], s5_b_re[0], s5_b_im[0],
                          s5_c_re[0], s5_c_im[0], s5_d[0])

    def mem_attend(z, mem, layer):
        kt, vt, vblk = mem
        n, l, _ = z.shape
        return memory_attend(z, kt, vt, layer, heads=mem_heads, f=d_mem, col_block=mem_col,
                             v_row_block=vblk).reshape(n * l, d_mem)

    def layer0(x, z_view, h0, mem, per_row_state):
        n, l, _ = x.shape
        h = x.reshape(n * l, d_model)
        z = rms_matmul(h, g_mix, w_in_b, 0).reshape(n, l, d_model)
        y_mem = mem_attend(z, mem, 0)
        y_s5, hend = s5_core(z.reshape(z_view), s5_ops, h0, width=width, per_row_state=per_row_state)
        h = layer_tail(h, y_s5.reshape(n * l, width), y_mem, w_out_b, g_mlp, w_up_b, w_down_b, 0,
                       glu=(w_glu_b[0], s5_b_glu[0]))
        return h, hend

    def layer1_front(h, n, l, mem, nb, seq_len):
        z = rms_matmul(h, g_mix, w_in_b, 1).reshape(n, l, d_model)
        y_mem = mem_attend(z, mem, 1)
        kv = shared_kv_t(h, norm_kv, wkv_t, wk, wf_t, bf_col, wf, bf_row, place, nb=nb, seq_len=seq_len)
        return z, y_mem, kv

    def layer1_back(h, y_fox, y_mem):
        return layer_tail(h, y_fox, y_mem, w_out_b, g_mlp, w_up_b, w_down_b, 1, final_g=norm_final)

    memkv_t = memory_kv_t(mem_prompt, w_memkv_t)
    m_tok = mem_prompt.shape[1]
    pmk_t = memkv_t[:, :, :d_mem]
    pmv_t = memkv_t[:, :, d_mem:]
    to_mem = lambda a: a.reshape(depth, n_p, mem_heads, mem_dh, m_tok).transpose(0, 1, 4, 2, 3)
    p_mem_k = to_mem(pmk_t)
    p_mem_v = to_mem(pmv_t)

    sw = 2 * per * s5_state
    zeros = jnp.zeros((width // LANES, n_p, 1, sw), F32)
    p_mem = (memkv_t, memkv_t, 1)
    h, p_hend = layer0(x_prompt, (n_p, seq, d_model), zeros, p_mem, False)
    p_hr, p_hi = s5_state_out(p_hend[:, :, 0], per, s5_state)
    z, y_mem, (kt, vt, vtb, ktok, lft, _, aq, ak) = layer1_front(h, n_p, seq, p_mem, n_p, seq)
    tokmaj = lambda a: a.reshape(n_p, seq, width)
    y_fox = fox_prompt(z, tokmaj(aq), tokmaj(ktok), tokmaj(ak), vtb, dh=fox_dh, width=width)
    y_prompt = layer1_back(h, y_fox.reshape(n_p * seq, width), y_mem).reshape(n_p, seq, d_model)
    to_heads = lambda a, n, l: a.reshape(n, fox_heads, fox_dh, l).transpose(0, 3, 1, 2)
    p_k = to_heads(kt, n_p, seq)
    p_v = to_heads(vt, n_p, seq)
    p_logf = lft[:, :fox_heads].transpose(0, 2, 1)

    s_mem = (mem_t(cache_mem_k), mem_t(cache_mem_v), 0)
    tok = n_s * dec_seq
    s_h0 = s5_state_in(state_s5_re[0], state_s5_im[0], per)[:, None]
    hs, s_hend = layer0(x_sample, (1, tok, d_model), s_h0, s_mem, True)
    s_hr, s_hi = s5_state_out(s_hend[:, 0], per, s5_state)
    zs, ys_mem, (kts, vts, _, _, lfts, cums, _, _) = layer1_front(hs, n_s, dec_seq, s_mem, 1, dec_seq)
    new_heads = lambda a: a[0].reshape(fox_heads, fox_dh, n_s, dec_seq).transpose(2, 3, 0, 1)
    s_k = new_heads(kts)
    s_v = new_heads(vts)
    s_logf = lfts[0, :fox_heads].reshape(fox_heads, n_s, dec_seq).transpose(1, 2, 0)

    new_t = jnp.stack([kts[0], vts[0]]).reshape(2, width, n_s, dec_seq).transpose(0, 2, 1, 3)
    new_t = jnp.pad(new_t, ((0, 0), (0, 0), (0, 0), (0, LANES - dec_seq))).astype(BF)
    c_new = cums[:, :fox_heads].reshape(n_s, dec_seq, fox_heads).transpose(0, 2, 1)
    qi = jnp.arange(dec_seq)[:, None]
    tp = jnp.arange(LANES)[None, :]
    c_pad = jnp.pad(c_new, ((0, 0), (0, 0), (0, LANES - dec_seq)))
    bias_new = jnp.where((tp <= qi)[None, None], -c_pad[:, :, None, :], NEG_INF).reshape(
        n_s, fox_heads * dec_seq, LANES)
    ys_fox = fox_decode(page_table, zs, new_t[0], new_t[1], bias_new, k_pages_t, v_pages_t, logf_pages,
                        heads=fox_heads, width=width).reshape(tok, width)
    y_sample = layer1_back(hs, ys_fox, ys_mem).reshape(n_s, dec_seq, d_model)

    return (y_prompt, y_sample, p_hr[None], p_hi[None], p_mem_k, p_mem_v, p_k, p_v, p_logf,
            s_hr[None], s_hi[None], s_k, s_v, s_logf)
```
